```python
import math
import jax, jax.numpy as jnp
from jax import lax
import numpy as np

D_MODEL = 4096
BATCH = 1
SEQ = 8192
DEPTH = 1

MIX_WIDTH = D_MODEL
MOBA_HEADS = 16
MOBA_HEAD_DIM = (MIX_WIDTH // 2) // MOBA_HEADS
MOBA_WIDTH = MOBA_HEADS * MOBA_HEAD_DIM
MOBA_BLOCK = 256
MOBA_TOPK = 3
MOBA_Q_CHUNK = 32
RET_HEADS = 8
RET_HEAD_DIM = (MIX_WIDTH - MOBA_WIDTH) // RET_HEADS
RET_WIDTH = RET_HEADS * RET_HEAD_DIM
RET_CHUNK = 512
ROPE_BASE = 10000.0
REL_BUCKETS = 32
REL_MAX_DIST = 128
IN_WIDTH = 3 * MOBA_WIDTH + 4 * RET_WIDTH
PEER_HEADS = 8
PEER_NKEYS = 128
PEER_EXPERTS = PEER_NKEYS * PEER_NKEYS
PEER_QDIM = 256
PEER_TOPK = 16
PEER_T_CHUNK = 64
NORM_EPS = 1e-6
GN_EPS = 1e-6
NEG = -1e30

kernel_name = 'hymba_moba_retnet_peer_layer'


def rmsnorm(x, g):
    xf = x.astype(jnp.float32)
    y = xf * lax.rsqrt(jnp.mean(xf * xf, axis=-1, keepdims=True) + NORM_EPS) * g.astype(jnp.float32)
    return y.astype(x.dtype)


def t5_bucket(n):
    n = jnp.maximum(n, 0)
    max_exact = REL_BUCKETS // 2
    nf = jnp.maximum(n, 1).astype(jnp.float32)
    large = max_exact + (jnp.log(nf / max_exact) / math.log(REL_MAX_DIST / max_exact)
                         * (REL_BUCKETS - max_exact)).astype(jnp.int32)
    large = jnp.minimum(large, REL_BUCKETS - 1)
    return jnp.where(n < max_exact, n, large)


def rotary(x, pos):
    half = x.shape[-1] // 2
    inv = ROPE_BASE ** (-jnp.arange(half, dtype=jnp.float32) / half)
    ang = pos.astype(jnp.float32)[:, None] * inv[None, :]
    cos = jnp.cos(ang)[None, :, None, :]
    sin = jnp.sin(ang)[None, :, None, :]
    xf = x.astype(jnp.float32)
    x1, x2 = xf[..., :half], xf[..., half:]
    return jnp.concatenate([x1 * cos - x2 * sin, x1 * sin + x2 * cos], axis=-1).astype(x.dtype)


def moba_attention(q, k, v, rel_bias):
    B, S, H, Dh = q.shape
    nblk = -(-S // MOBA_BLOCK)
    n_sel = min(MOBA_TOPK, nblk)
    Lp = nblk * MOBA_BLOCK
    pad = ((0, 0), (0, Lp - S), (0, 0), (0, 0))
    q, k, v = jnp.pad(q, pad), jnp.pad(k, pad), jnp.pad(v, pad)
    kb = k.reshape(B, nblk, MOBA_BLOCK, H, Dh)
    vb = v.reshape(B, nblk, MOBA_BLOCK, H, Dh)
    kmean = jnp.mean(kb.astype(jnp.float32), axis=2)
    kbh = kb.transpose(0, 3, 1, 2, 4)
    vbh = vb.transpose(0, 3, 1, 2, 4)
    scale = Dh ** -0.5
    n_chunks = Lp // MOBA_Q_CHUNK
    qc = q.reshape(B, n_chunks, MOBA_Q_CHUNK, H, Dh).transpose(1, 0, 2, 3, 4)
    blk_ids = jnp.arange(nblk)
    offs = jnp.arange(MOBA_BLOCK)
    bidx = jnp.arange(B)[:, None, None, None]
    hidx = jnp.arange(H)[None, None, :, None]

    def one_chunk(args):
        qi, ci = args
        start = ci * MOBA_Q_CHUNK
        cur = start // MOBA_BLOCK
        qpos = start + jnp.arange(MOBA_Q_CHUNK)
        qf = qi.astype(jnp.float32)
        gate = jnp.einsum('bqhd,bnhd->bqhn', qf, kmean)
        gate = jnp.where(blk_ids < cur, gate, NEG)
        _, sel = lax.top_k(gate, n_sel)
        sel_valid = sel < cur
        k_sel = kbh[bidx, hidx, sel]
        v_sel = vbh[bidx, hidx, sel]
        s_sel = jnp.einsum('bqhd,bqhjkd->bqhjk', qf * scale, k_sel.astype(jnp.float32))
        rel_sel = qpos[None, :, None, None, None] - (sel[..., None] * MOBA_BLOCK + offs)
        s_sel = s_sel + rel_bias[t5_bucket(rel_sel), hidx[..., None]].astype(jnp.float32)
        s_sel = jnp.where(sel_valid[..., None], s_sel, NEG)
        s_sel = s_sel.reshape(B, MOBA_Q_CHUNK, H, n_sel * MOBA_BLOCK)
        k_own = lax.dynamic_index_in_dim(kbh, cur, axis=2, keepdims=False)
        v_own = lax.dynamic_index_in_dim(vbh, cur, axis=2, keepdims=False)
        s_own = jnp.einsum('bqhd,bhkd->bqhk', qf * scale, k_own.astype(jnp.float32))
        rel_own = qpos[:, None] - (cur * MOBA_BLOCK + offs)[None, :]
        s_own = s_own + rel_bias[t5_bucket(rel_own)].transpose(0, 2, 1)[None].astype(jnp.float32)
        s_own = jnp.where((rel_own >= 0)[None, :, None, :], s_own, NEG)
        p = jax.nn.softmax(jnp.concatenate([s_sel, s_own], axis=-1), axis=-1)
        p_sel = p[..., :n_sel * MOBA_BLOCK].reshape(B, MOBA_Q_CHUNK, H, n_sel, MOBA_BLOCK)
        p_own = p[..., n_sel * MOBA_BLOCK:]
        out = (jnp.einsum('bqhjk,bqhjkd->bqhd', p_sel, v_sel.astype(jnp.float32))
               + jnp.einsum('bqhk,bhkd->bqhd', p_own, v_own.astype(jnp.float32)))
        return out.astype(q.dtype)

    out = lax.map(one_chunk, (qc, jnp.arange(n_chunks)))
    return out.transpose(1, 0, 2, 3, 4).reshape(B, Lp, H, Dh)[:, :S]


def retention(q, k, v):
    B, S, H, D = q.shape
    C = RET_CHUNK
    n = -(-S // C)
    Lp = n * C
    pad = ((0, 0), (0, Lp - S), (0, 0), (0, 0))
    q, k, v = (jnp.pad(t.astype(jnp.float32), pad) for t in (q, k, v))
    log_g = jnp.log1p(-(2.0 ** (-5.0 - jnp.arange(H, dtype=jnp.float32))))
    idx = jnp.arange(C, dtype=jnp.float32)
    diff = idx[:, None] - idx[None, :]
    decay_intra = jnp.where(diff >= 0, jnp.exp(log_g[:, None, None] * jnp.maximum(diff, 0.0)), 0.0)
    q_decay = jnp.exp(log_g[:, None] * (idx + 1.0))[None, :, :, None]
    k_decay = jnp.exp(log_g[:, None] * (C - 1.0 - idx))[None, :, :, None]
    chunk_decay = jnp.exp(log_g * C)[None, :, None, None]
    to_chunks = lambda t: t.reshape(B, n, C, H, D).transpose(1, 0, 3, 2, 4)

    def step(state, inp):
        qc, kc, vc = inp
        inner = jnp.einsum('bhid,bhjd->bhij', qc, kc) * decay_intra[None]
        o = (jnp.einsum('bhij,bhjd->bhid', inner, vc)
             + jnp.einsum('bhid,bhde->bhie', qc * q_decay, state))
        state = state * chunk_decay + jnp.einsum('bhjd,bhje->bhde', kc * k_decay, vc)
        return state, o

    state0 = jnp.zeros((B, H, D, D), jnp.float32)
    _, o = lax.scan(step, state0, (to_chunks(q), to_chunks(k), to_chunks(v)))
    return o.transpose(1, 0, 3, 2, 4).reshape(B, Lp, H, D)[:, :S]


def head_group_norm(o):
    mu = jnp.mean(o, axis=-1, keepdims=True)
    var = jnp.mean(jnp.square(o - mu), axis=-1, keepdims=True)
    return (o - mu) * lax.rsqrt(var + GN_EPS)


def peer_ffn(xn, w_q, sub_keys, u, v):
    B, S, Dm = xn.shape
    T = B * S
    tok = xn.reshape(T, Dm)
    q = (tok @ w_q).astype(jnp.float32).reshape(T, PEER_HEADS, 2, PEER_QDIM // 2)
    s = jnp.einsum('thcd,hcnd->thcn', q, sub_keys.astype(jnp.float32))
    sv, si = lax.top_k(s, PEER_TOPK)
    cand_s = (sv[..., 0, :, None] + sv[..., 1, None, :]).reshape(T, PEER_HEADS, PEER_TOPK * PEER_TOPK)
    cand_e = (si[..., 0, :, None] * PEER_NKEYS + si[..., 1, None, :]).reshape(T, PEER_HEADS, PEER_TOPK * PEER_TOPK)
    top_s, top_pos = lax.top_k(cand_s, PEER_TOPK)
    experts = jnp.take_along_axis(cand_e, top_pos, axis=-1)
    gates = jax.nn.softmax(top_s, axis=-1)
    n_chunks = -(-T // PEER_T_CHUNK)
    Tp = n_chunks * PEER_T_CHUNK
    tok_p = jnp.pad(tok, ((0, Tp - T), (0, 0))).reshape(n_chunks, PEER_T_CHUNK, Dm)
    exp_p = jnp.pad(experts, ((0, Tp - T), (0, 0), (0, 0))).reshape(n_chunks, PEER_T_CHUNK, PEER_HEADS, PEER_TOPK)
    gate_p = jnp.pad(gates, ((0, Tp - T), (0, 0), (0, 0))).reshape(n_chunks, PEER_T_CHUNK, PEER_HEADS, PEER_TOPK)

    def one_chunk(args):
        tc, ec, gc = args
        u_sel = u[ec]
        act = jax.nn.gelu(jnp.einsum('td,thkd->thk', tc, u_sel).astype(jnp.float32), approximate=False)
        w = (act * gc).astype(v.dtype)
        return jnp.einsum('thk,thkd->td', w, v[ec])

    out = lax.map(one_chunk, (tok_p, exp_p, gate_p))
    return out.reshape(Tp, Dm)[:T].reshape(B, S, Dm).astype(xn.dtype)


def setup_inputs(seed: int = 0) -> dict:
    key = jax.random.key(seed)
    ks = jax.random.split(key, 12)
    f32 = jnp.float32
    x = jax.random.normal(ks[0], (BATCH, SEQ, D_MODEL), f32)
    norm_mix_g = 1.0 + 0.02 * jax.random.normal(ks[1], (DEPTH, D_MODEL), f32)
    w_in = jax.random.normal(ks[2], (DEPTH, D_MODEL, IN_WIDTH), f32) * D_MODEL ** -0.5
    w_out = jax.random.normal(ks[3], (DEPTH, MIX_WIDTH, D_MODEL), f32) * MIX_WIDTH ** -0.5
    rel_bias = 0.5 * jax.random.normal(ks[4], (REL_BUCKETS, MOBA_HEADS), f32)
    norm_ffn_g = 1.0 + 0.02 * jax.random.normal(ks[5], (DEPTH, D_MODEL), f32)
    peer_w_q = jax.random.normal(ks[6], (DEPTH, D_MODEL, PEER_HEADS * PEER_QDIM), f32) * D_MODEL ** -0.5
    peer_sub_keys = jax.random.normal(ks[7], (DEPTH, PEER_HEADS, 2, PEER_NKEYS, PEER_QDIM // 2), f32) * (PEER_QDIM // 2) ** -0.5
    peer_u = jax.random.normal(ks[8], (DEPTH, PEER_EXPERTS, D_MODEL), f32) * D_MODEL ** -0.5
    peer_v = jax.random.normal(ks[9], (DEPTH, PEER_EXPERTS, D_MODEL), f32) * PEER_HEADS ** -0.5
    norm_final_g = 1.0 + 0.02 * jax.random.normal(ks[10], (D_MODEL,), f32)
    return {'x': x, 'norm_mix_g': norm_mix_g, 'w_in': w_in, 'w_out': w_out, 'rel_bias': rel_bias,
            'norm_ffn_g': norm_ffn_g, 'peer_w_q': peer_w_q, 'peer_sub_keys': peer_sub_keys,
            'peer_u': peer_u, 'peer_v': peer_v, 'norm_final_g': norm_final_g}


def reference(x, norm_mix_g, w_in, w_out, rel_bias, norm_ffn_g, peer_w_q, peer_sub_keys,
              peer_u, peer_v, norm_final_g):
    B, S, _ = x.shape
    pos = jnp.arange(S, dtype=jnp.int32)
    splits = [MOBA_WIDTH, 2 * MOBA_WIDTH, 3 * MOBA_WIDTH, 3 * MOBA_WIDTH + RET_WIDTH,
              3 * MOBA_WIDTH + 2 * RET_WIDTH, 3 * MOBA_WIDTH + 3 * RET_WIDTH]
    h = x
    for layer in range(DEPTH):
        n = rmsnorm(h, norm_mix_g[layer])
        proj = n @ w_in[layer]
        mq, mk, mv, rq, rk, rv, rg = jnp.split(proj, splits, axis=-1)
        mshape = (B, S, MOBA_HEADS, MOBA_HEAD_DIM)
        a_out = moba_attention(mq.reshape(mshape), mk.reshape(mshape), mv.reshape(mshape), rel_bias)
        rshape = (B, S, RET_HEADS, RET_HEAD_DIM)
        rq = rotary(rq.reshape(rshape), pos)
        rk = rotary(rk.reshape(rshape), pos) * (RET_HEAD_DIM ** -0.5)
        r_out = head_group_norm(retention(rq, rk, rv.reshape(rshape))).reshape(B, S, RET_WIDTH)
        r_out = jax.nn.silu(rg.astype(jnp.float32)) * r_out
        mix = jnp.concatenate([a_out.reshape(B, S, MOBA_WIDTH).astype(h.dtype), r_out.astype(h.dtype)], axis=-1)
        h = h + mix @ w_out[layer]
        h = h + peer_ffn(rmsnorm(h, norm_ffn_g[layer]), peer_w_q[layer], peer_sub_keys[layer],
                         peer_u[layer], peer_v[layer])
    return rmsnorm(h, norm_final_g)
```

```python
import functools
import math

import numpy as np
import jax
import jax.numpy as jnp
from jax import lax
from jax.experimental import pallas as pl
from jax.experimental.pallas import tpu as pltpu

D_MODEL = 4096
MOBA_HEADS = 16
MOBA_HEAD_DIM = 128
MOBA_WIDTH = MOBA_HEADS * MOBA_HEAD_DIM
MOBA_BLOCK = 256
MOBA_TOPK = 3
RET_HEADS = 8
RET_HEAD_DIM = 256
RET_WIDTH = RET_HEADS * RET_HEAD_DIM
RET_CHUNK = 512
ROPE_BASE = 10000.0
REL_BUCKETS = 32
REL_MAX_DIST = 128
IN_WIDTH = 3 * MOBA_WIDTH + 4 * RET_WIDTH
PEER_HEADS = 8
PEER_NKEYS = 128
PEER_EXPERTS = PEER_NKEYS * PEER_NKEYS
PEER_QDIM = 256
PEER_TOPK = 16
NORM_EPS = 1e-6
GN_EPS = 1e-6
NEG = -1e30

LANES = 128
N_SLABS = IN_WIDTH // LANES
VMEM_LIMIT = 56 * 1024 * 1024

SLAB_MQ = 0
SLAB_MK = MOBA_WIDTH // LANES
SLAB_MV = 2 * MOBA_WIDTH // LANES
SLAB_RQ = 3 * MOBA_WIDTH // LANES
SLAB_RK = SLAB_RQ + RET_WIDTH // LANES
SLAB_RV = SLAB_RK + RET_WIDTH // LANES
SLAB_RG = SLAB_RV + RET_WIDTH // LANES


def _params(sem, vmem=VMEM_LIMIT):
    return pltpu.CompilerParams(dimension_semantics=sem, vmem_limit_bytes=vmem)


def _dot(a, b):
    return jnp.dot(a, b, preferred_element_type=jnp.float32)


def _dot_nt(a, b):
    return lax.dot_general(a, b, (((1,), (1,)), ((), ())), preferred_element_type=jnp.float32)


def _rmsnorm_kernel(x_ref, g_ref, o_ref, *, transpose):
    x = x_ref[...]
    y = x * lax.rsqrt(jnp.mean(x * x, axis=-1, keepdims=True) + NORM_EPS) * g_ref[...]
    if transpose:
        y = y.T
    o_ref[...] = y.astype(o_ref.dtype)


def _rmsnorm(x, g, *, transpose, tm=256):
    s, d = x.shape
    if transpose:
        out_shape = jax.ShapeDtypeStruct((d, s), jnp.bfloat16)
        out_spec = pl.BlockSpec((d, tm), lambda i: (0, i))
    else:
        out_shape = jax.ShapeDtypeStruct((s, d), jnp.bfloat16)
        out_spec = pl.BlockSpec((tm, d), lambda i: (i, 0))
    return pl.pallas_call(
        functools.partial(_rmsnorm_kernel, transpose=transpose),
        grid=(s // tm,),
        in_specs=[pl.BlockSpec((tm, d), lambda i: (i, 0)),
                  pl.BlockSpec((1, d), lambda i: (0, 0))],
        out_specs=out_spec,
        out_shape=out_shape,
        compiler_params=_params(("parallel",)),
        name="rmsnorm_t" if transpose else "rmsnorm",
    )(x, g.reshape(1, d))


def _inproj_kernel(a_ref, b_ref, o_ref):
    acc = _dot(a_ref[...], b_ref[...])
    for c in range(o_ref.shape[0]):
        o_ref[c] = acc[:, c * LANES:(c + 1) * LANES].astype(o_ref.dtype)


def _inproj(xn, w, *, tm=512, tn=1024):
    s, d = xn.shape
    n = w.shape[1]
    return pl.pallas_call(
        _inproj_kernel,
        grid=(n // tn, s // tm),
        in_specs=[pl.BlockSpec((tm, d), lambda j, i: (i, 0)),
                  pl.BlockSpec((d, tn), lambda j, i: (0, j))],
        out_specs=pl.BlockSpec((tn // LANES, tm, LANES), lambda j, i: (j, i, 0)),
        out_shape=jax.ShapeDtypeStruct((n // LANES, s, LANES), jnp.bfloat16),
        compiler_params=_params(("parallel", "parallel")),
        name="inproj",
    )(xn, w)


def _kmean_kernel(k_ref, o_ref):
    k = k_ref[0].astype(jnp.float32)
    nblk = k.shape[0] // MOBA_BLOCK
    km = jnp.mean(k.reshape(nblk, MOBA_BLOCK, LANES), axis=1)
    pad = jnp.zeros((LANES - nblk, LANES), jnp.float32)
    o_ref[0] = jnp.concatenate([km, pad], axis=0)


def _kmean(proj):
    s = proj.shape[1]
    return pl.pallas_call(
        _kmean_kernel,
        grid=(MOBA_HEADS,),
        in_specs=[pl.BlockSpec((1, s, LANES), lambda h: (SLAB_MK + h, 0, 0))],
        out_specs=pl.BlockSpec((1, LANES, LANES), lambda h: (h, 0, 0)),
        out_shape=jax.ShapeDtypeStruct((MOBA_HEADS, LANES, LANES), jnp.float32),
        compiler_params=_params(("parallel",)),
        name="moba_kmean",
    )(proj)


def _t5_bucket_starts():
    max_exact = REL_BUCKETS // 2
    d = np.arange(0, REL_MAX_DIST + 1)
    nf = np.maximum(d, 1).astype(np.float64)
    large = max_exact + (np.log(nf / max_exact) / math.log(REL_MAX_DIST / max_exact)
                         * (REL_BUCKETS - max_exact)).astype(np.int64)
    large = np.minimum(large, REL_BUCKETS - 1)
    bucket = np.where(d < max_exact, d, large)
    return [int(np.argmax(bucket >= b)) for b in range(REL_BUCKETS)]


_BUCKET_START = _t5_bucket_starts()


def _bias_kernel(rb_ref, o_ref):
    h = pl.program_id(0)
    qi = lax.broadcasted_iota(jnp.int32, (MOBA_BLOCK, MOBA_BLOCK), 0)
    ki = lax.broadcasted_iota(jnp.int32, (MOBA_BLOCK, MOBA_BLOCK), 1)
    for t, off in enumerate((MOBA_BLOCK, 0)):
        d = qi - ki + off
        dd = jnp.maximum(d, 0)
        bias = jnp.full((MOBA_BLOCK, MOBA_BLOCK), rb_ref[0, h], jnp.float32)
        for b in range(1, REL_BUCKETS):
            bias = jnp.where(dd >= _BUCKET_START[b], rb_ref[b, h], bias)
        o_ref[0, t] = jnp.where(d >= 0, bias, NEG)


def _bias_tables(rel_bias):
    return pl.pallas_call(
        _bias_kernel,
        grid=(MOBA_HEADS,),
        in_specs=[pl.BlockSpec(memory_space=pltpu.SMEM)],
        out_specs=pl.BlockSpec((1, 2, MOBA_BLOCK, MOBA_BLOCK), lambda h: (h, 0, 0, 0)),
        out_shape=jax.ShapeDtypeStruct((MOBA_HEADS, 2, MOBA_BLOCK, MOBA_BLOCK), jnp.float32),
        compiler_params=_params(("arbitrary",)),
        name="moba_bias",
    )(rel_bias)


def _moba_kernel(rb_ref, q_ref, k_ref, v_ref, km_ref, tbl_ref, o_ref):
    h = pl.program_id(0)
    qi = pl.program_id(1)
    blk = MOBA_BLOCK
    scale = MOBA_HEAD_DIM ** -0.5
    q = q_ref[0]
    far = rb_ref[REL_BUCKETS - 1, h]

    gate = _dot_nt(q, km_ref[0].astype(jnp.bfloat16))
    col = lax.broadcasted_iota(jnp.int32, (blk, LANES), 1).astype(jnp.float32)
    valid = col < qi.astype(jnp.float32)
    g = jnp.where(valid, gate, NEG)
    sel = jnp.zeros((blk, LANES), jnp.float32)
    for _ in range(MOBA_TOPK):
        m = jnp.max(g, axis=1, keepdims=True)
        idx = jnp.min(jnp.where(g == m, col, float(LANES)), axis=1, keepdims=True)
        pick = col == idx
        sel = jnp.where(pick, 1.0, sel)
        g = jnp.where(pick, -jnp.inf, g)
    mask_bias = jnp.where((sel > 0.5) & valid, 0.0, NEG).astype(jnp.bfloat16)
    q_aug = jnp.concatenate([q, mask_bias], axis=1)
    lane = lax.broadcasted_iota(jnp.int32, (blk, LANES), 1)

    def kv(j):
        start = pl.multiple_of(j * blk, blk)
        return k_ref[0, pl.ds(start, blk), :], v_ref[0, pl.ds(start, blk), :]

    def masked_scores(j, kj):
        onehot = (lane == j).astype(jnp.bfloat16)
        return _dot_nt(q_aug, jnp.concatenate([kj, onehot], axis=1)) * scale

    def update(carry, s, vj):
        m_old, l_old, acc = carry
        m_new = jnp.maximum(m_old, jnp.max(s, axis=1, keepdims=True))
        alpha = jnp.exp(m_old - m_new)
        p = jnp.exp(s - m_new)
        l_new = alpha * l_old + jnp.sum(p, axis=1, keepdims=True)
        acc = alpha * acc + _dot(p.astype(jnp.bfloat16), vj)
        return m_new, l_new, acc

    k_own, v_own = kv(qi)
    s = _dot_nt(q, k_own) * scale + tbl_ref[0, 1]
    m0 = jnp.max(s, axis=1, keepdims=True)
    p = jnp.exp(s - m0)
    carry = (m0, jnp.sum(p, axis=1, keepdims=True), _dot(p.astype(jnp.bfloat16), v_own))

    jp = jnp.maximum(qi - 1, 0)
    k_p, v_p = kv(jp)
    carry = update(carry, masked_scores(jp, k_p) + tbl_ref[0, 0], v_p)

    def body(j, c):
        kj, vj = kv(j)
        return update(c, masked_scores(j, kj) + far, vj)

    _, l_fin, acc = lax.fori_loop(0, jnp.maximum(qi - 1, 0), body, carry)
    o_ref[...] = (acc / l_fin).astype(o_ref.dtype)


def _moba(proj, kmean, tables, rel_bias):
    s = proj.shape[1]
    nq = s // MOBA_BLOCK
    return pl.pallas_call(
        _moba_kernel,
        grid=(MOBA_HEADS, nq),
        in_specs=[pl.BlockSpec(memory_space=pltpu.SMEM),
                  pl.BlockSpec((1, MOBA_BLOCK, LANES), lambda h, i: (SLAB_MQ + h, i, 0)),
                  pl.BlockSpec((1, s, LANES), lambda h, i: (SLAB_MK + h, 0, 0)),
                  pl.BlockSpec((1, s, LANES), lambda h, i: (SLAB_MV + h, 0, 0)),
                  pl.BlockSpec((1, LANES, LANES), lambda h, i: (h, 0, 0)),
                  pl.BlockSpec((1, 2, MOBA_BLOCK, MOBA_BLOCK), lambda h, i: (h, 0, 0, 0))],
        out_specs=pl.BlockSpec((MOBA_BLOCK, LANES), lambda h, i: (i, h)),
        out_shape=jax.ShapeDtypeStruct((s, MOBA_WIDTH), jnp.bfloat16),
        compiler_params=_params(("parallel", "parallel")),
        name="moba_attn",
    )(rel_bias, proj, proj, proj, kmean, tables)


def _retention_kernel(lg_ref, q_ref, k_ref, v_ref, g_ref, cos_ref, sin_ref, o_ref,
                      state_ref, decay_ref):
    h = pl.program_id(0)
    c = pl.program_id(1)
    cs = RET_CHUNK
    lg = lg_ref[h]

    @pl.when(c == 0)
    def _():
        state_ref[...] = jnp.zeros_like(state_ref)
        i = lax.broadcasted_iota(jnp.int32, (cs, cs), 0)
        j = lax.broadcasted_iota(jnp.int32, (cs, cs), 1)
        diff = (i - j).astype(jnp.float32)
        decay_ref[...] = jnp.where(diff >= 0, jnp.exp(lg * jnp.maximum(diff, 0.0)), 0.0)

    cos = cos_ref[...]
    sin = sin_ref[...]

    def rope(ref):
        x1 = ref[0].astype(jnp.float32)
        x2 = ref[1].astype(jnp.float32)
        return jnp.concatenate([x1 * cos - x2 * sin, x1 * sin + x2 * cos], axis=1)

    rq = rope(q_ref)
    rk = rope(k_ref) * (RET_HEAD_DIM ** -0.5)
    v = jnp.concatenate([v_ref[0], v_ref[1]], axis=1)
    row = lax.broadcasted_iota(jnp.int32, (cs, 1), 0).astype(jnp.float32)
    q_decay = jnp.exp(lg * (row + 1.0))
    k_decay = jnp.exp(lg * (cs - 1.0 - row))

    state = state_ref[...]
    inner = _dot_nt(rq.astype(jnp.bfloat16), rk.astype(jnp.bfloat16)) * decay_ref[...]
    o = (_dot(inner.astype(jnp.bfloat16), v)
         + _dot((rq * q_decay).astype(jnp.bfloat16), state.astype(jnp.bfloat16)))
    kd_t = (rk * k_decay).T.astype(jnp.bfloat16)
    state_ref[...] = state * jnp.exp(lg * cs) + _dot(kd_t, v)

    mu = jnp.mean(o, axis=-1, keepdims=True)
    ctr = o - mu
    var = jnp.mean(ctr * ctr, axis=-1, keepdims=True)
    y = ctr * lax.rsqrt(var + GN_EPS)
    gate = jnp.concatenate([g_ref[0], g_ref[1]], axis=1).astype(jnp.float32)
    o_ref[...] = (gate * jax.nn.sigmoid(gate) * y).astype(o_ref.dtype)


def _retention(proj, log_g, cos, sin):
    s = proj.shape[1]
    cs = RET_CHUNK
    pair = lambda base: pl.BlockSpec((2, cs, LANES), lambda h, c: (base // 2 + h, c, 0))
    return pl.pallas_call(
        _retention_kernel,
        grid=(RET_HEADS, s // cs),
        in_specs=[pl.BlockSpec(memory_space=pltpu.SMEM),
                  pair(SLAB_RQ), pair(SLAB_RK), pair(SLAB_RV), pair(SLAB_RG),
                  pl.BlockSpec((cs, LANES), lambda h, c: (c, 0)),
                  pl.BlockSpec((cs, LANES), lambda h, c: (c, 0))],
        out_specs=pl.BlockSpec((cs, RET_HEAD_DIM), lambda h, c: (c, h)),
        out_shape=jax.ShapeDtypeStruct((s, RET_WIDTH), jnp.bfloat16),
        scratch_shapes=[pltpu.VMEM((RET_HEAD_DIM, RET_HEAD_DIM), jnp.float32),
                        pltpu.VMEM((cs, cs), jnp.float32)],
        compiler_params=_params(("parallel", "arbitrary")),
        name="retention",
    )(log_g, proj, proj, proj, proj, cos, sin)


def _outproj_kernel(a_ref, r_ref, wa_ref, wr_ref, x_ref, o_ref):
    o_ref[...] = x_ref[...] + _dot(a_ref[...], wa_ref[...]) + _dot(r_ref[...], wr_ref[...])


def _outproj(a_out, r_out, w_out, x, *, tm=512, tn=1024):
    s, d = x.shape
    ka = a_out.shape[1]
    kr = r_out.shape[1]
    return pl.pallas_call(
        _outproj_kernel,
        grid=(d // tn, s // tm),
        in_specs=[pl.BlockSpec((tm, ka), lambda j, i: (i, 0)),
                  pl.BlockSpec((tm, kr), lambda j, i: (i, 0)),
                  pl.BlockSpec((ka, tn), lambda j, i: (0, j)),
                  pl.BlockSpec((kr, tn), lambda j, i: (ka // kr, j)),
                  pl.BlockSpec((tm, tn), lambda j, i: (i, j))],
        out_specs=pl.BlockSpec((tm, tn), lambda j, i: (i, j)),
        out_shape=jax.ShapeDtypeStruct((s, d), jnp.float32),
        compiler_params=_params(("parallel", "parallel")),
        name="outproj",
    )(a_out, r_out, w_out, w_out, x)


def _knockout_topk(s, k, on_value):
    r = s.shape[0]
    rid = lax.broadcasted_iota(jnp.int32, s.shape, 0).astype(jnp.float32)
    for i in range(k):
        m = jnp.max(s, axis=0, keepdims=True)
        on_value(i, m)
        if i + 1 < k:
            idx = jnp.min(jnp.where(s == m, rid, float(r)), axis=0, keepdims=True)
            s = jnp.where(rid == idx, -jnp.inf, s)


def _peer_prep_kernel(xt_ref, wq_ref, keys_ref, s0_ref, s1_ref, thr_ref, nrm_ref, a_ref, b_ref):
    nk = PEER_NKEYS
    qt = _dot(wq_ref[...], xt_ref[...]).astype(jnp.bfloat16)
    s0 = _dot(keys_ref[0], qt[:nk])
    s1 = _dot(keys_ref[1], qt[nk:])
    s0_ref[0] = s0
    s1_ref[0] = s1

    def put(ref):
        def f(i, m):
            ref[i:i + 1, :] = m
        return f

    _knockout_topk(s0, PEER_TOPK, put(a_ref))
    _knockout_topk(s1, PEER_TOPK, put(b_ref))
    a = a_ref[...]
    b = b_ref[...]
    half = PEER_TOPK // 2
    cands = [a[0:1] + b]
    for i in range(1, half):
        cands.append(a[i:i + 1] + b[0:half])
    cands.append(a[half:] + b[0:1])
    cand = jnp.concatenate(cands, axis=0)

    top = a[0:1] + b[0:1]
    stats = {"z": jnp.zeros_like(top), "thr": top}

    def acc(i, m):
        stats["z"] = stats["z"] + jnp.exp(m - top)
        stats["thr"] = m

    _knockout_topk(cand, PEER_TOPK, acc)
    thr_ref[0] = stats["thr"]
    nrm_ref[0] = top + jnp.log(stats["z"])


def _peer_prep(xnt, wqt, keys, *, tt=512):
    d, t = xnt.shape
    nk = PEER_NKEYS
    sc_spec = pl.BlockSpec((1, nk, tt), lambda i, h: (h, 0, i))
    row_spec = pl.BlockSpec((1, 1, tt), lambda i, h: (h, 0, i))
    sc_shape = jax.ShapeDtypeStruct((PEER_HEADS, nk, t), jnp.float32)
    row_shape = jax.ShapeDtypeStruct((PEER_HEADS, 1, t), jnp.float32)
    return pl.pallas_call(
        _peer_prep_kernel,
        grid=(t // tt, PEER_HEADS),
        in_specs=[pl.BlockSpec((d, tt), lambda i, h: (0, i)),
                  pl.BlockSpec((PEER_QDIM, d), lambda i, h: (h, 0)),
                  pl.BlockSpec((2, nk, nk), lambda i, h: (h, 0, 0))],
        out_specs=[sc_spec, sc_spec, row_spec, row_spec],
        out_shape=[sc_shape, sc_shape, row_shape, row_shape],
        scratch_shapes=[pltpu.VMEM((PEER_TOPK, tt), jnp.float32),
                        pltpu.VMEM((PEER_TOPK, tt), jnp.float32)],
        compiler_params=_params(("parallel", "arbitrary")),
        name="peer_prep",
    )(xnt, wqt, keys)


def _peer_kernel(xt_ref, u_ref, vt_ref, s0_ref, s1_ref, thr_ref, nrm_ref, o_ref):
    et = pl.program_id(1)
    nk = PEER_NKEYS
    rows = u_ref.shape[0] // nk
    pre = _dot(u_ref[...], xt_ref[...])
    act = 0.5 * pre * (1.0 + lax.erf(pre * math.sqrt(0.5)))
    pieces = []
    for r in range(rows):
        i = et * rows + r
        w = jnp.zeros((nk, xt_ref.shape[1]), jnp.float32)
        for h in range(PEER_HEADS):
            z = s0_ref[h, pl.ds(i, 1), :] + s1_ref[h]
            w = w + jnp.where(z >= thr_ref[h], jnp.exp(z - nrm_ref[h]), 0.0)
        pieces.append((act[r * nk:(r + 1) * nk] * w).astype(jnp.bfloat16))
    contrib = _dot(vt_ref[...], jnp.concatenate(pieces, axis=0))

    @pl.when(et == 0)
    def _():
        o_ref[...] = contrib

    @pl.when(et > 0)
    def _():
        o_ref[...] += contrib


def _peer(xnt, u, vt, s0, s1, thr, nrm, *, tt=512, te=512):
    d, t = xnt.shape
    e = u.shape[0]
    nk = PEER_NKEYS
    full = pl.BlockSpec((PEER_HEADS, nk, tt), lambda i, j: (0, 0, i))
    rowv = pl.BlockSpec((PEER_HEADS, 1, tt), lambda i, j: (0, 0, i))
    return pl.pallas_call(
        _peer_kernel,
        grid=(t // tt, e // te),
        in_specs=[pl.BlockSpec((d, tt), lambda i, j: (0, i)),
                  pl.BlockSpec((te, d), lambda i, j: (j, 0)),
                  pl.BlockSpec((d, te), lambda i, j: (0, j)),
                  full, full, rowv, rowv],
        out_specs=pl.BlockSpec((d, tt), lambda i, j: (0, i)),
        out_shape=jax.ShapeDtypeStruct((d, t), jnp.float32),
        compiler_params=_params(("parallel", "arbitrary")),
        name="peer_experts",
    )(xnt, u, vt, s0, s1, thr, nrm)


def _final_kernel(h_ref, pt_ref, g_ref, o_ref):
    x = h_ref[...] + pt_ref[...].T
    o_ref[...] = x * lax.rsqrt(jnp.mean(x * x, axis=-1, keepdims=True) + NORM_EPS) * g_ref[...]


def _final(h, peer_t, g, *, tm=256):
    s, d = h.shape
    return pl.pallas_call(
        _final_kernel,
        grid=(s // tm,),
        in_specs=[pl.BlockSpec((tm, d), lambda i: (i, 0)),
                  pl.BlockSpec((d, tm), lambda i: (0, i)),
                  pl.BlockSpec((1, d), lambda i: (0, 0))],
        out_specs=pl.BlockSpec((tm, d), lambda i: (i, 0)),
        out_shape=jax.ShapeDtypeStruct((s, d), jnp.float32),
        compiler_params=_params(("parallel",)),
        name="final_norm",
    )(h, peer_t, g.reshape(1, d))


def kernel(x, norm_mix_g, w_in, w_out, rel_bias, norm_ffn_g, peer_w_q, peer_sub_keys,
           peer_u, peer_v, norm_final_g):
    b, s, d = x.shape
    assert b == 1 and d == D_MODEL and s % RET_CHUNK == 0 and norm_mix_g.shape[0] == 1
    bf16 = jnp.bfloat16
    h = x.reshape(s, d)

    half = RET_HEAD_DIM // 2
    inv = ROPE_BASE ** (-jnp.arange(half, dtype=jnp.float32) / half)
    ang = jnp.arange(s, dtype=jnp.int32).astype(jnp.float32)[:, None] * inv[None, :]
    cos, sin = jnp.cos(ang), jnp.sin(ang)
    log_g = jnp.log1p(-(2.0 ** (-5.0 - jnp.arange(RET_HEADS, dtype=jnp.float32))))

    xn = _rmsnorm(h, norm_mix_g[0], transpose=False)
    proj = _inproj(xn, w_in[0].astype(bf16))
    a_out = _moba(proj, _kmean(proj), _bias_tables(rel_bias), rel_bias)
    r_out = _retention(proj, log_g, cos, sin)
    h = _outproj(a_out, r_out, w_out[0].astype(bf16), h)

    xnt = _rmsnorm(h, norm_ffn_g[0], transpose=True)
    keys = peer_sub_keys[0].astype(bf16).reshape(PEER_HEADS * 2, PEER_NKEYS, PEER_QDIM // 2)
    s0, s1, thr, nrm = _peer_prep(xnt, peer_w_q[0].T.astype(bf16), keys)
    peer_t = _peer(xnt, peer_u[0].astype(bf16), peer_v[0].T.astype(bf16), s0, s1, thr, nrm)
    y = _final(h, peer_t, norm_final_g)
    return y.reshape(b, s, d)
```

```python
import functools
import math

import numpy as np
import jax
import jax.numpy as jnp
from jax import lax
from jax.experimental import pallas as pl
from jax.experimental.pallas import tpu as pltpu

D_MODEL = 4096
MOBA_HEADS = 16
MOBA_HEAD_DIM = 128
MOBA_WIDTH = MOBA_HEADS * MOBA_HEAD_DIM
MOBA_BLOCK = 256
MOBA_TOPK = 3
RET_HEADS = 8
RET_HEAD_DIM = 256
RET_WIDTH = RET_HEADS * RET_HEAD_DIM
RET_CHUNK = 512
ROPE_BASE = 10000.0
REL_BUCKETS = 32
REL_MAX_DIST = 128
IN_WIDTH = 3 * MOBA_WIDTH + 4 * RET_WIDTH
PEER_HEADS = 8
PEER_NKEYS = 128
PEER_EXPERTS = PEER_NKEYS * PEER_NKEYS
PEER_QDIM = 256
PEER_TOPK = 16
NORM_EPS = 1e-6
GN_EPS = 1e-6
NEG = -1e30

LOG2E = math.log2(math.e)
MOBA_HEADS_PER_STEP = 2

LANES = 128
N_SLABS = IN_WIDTH // LANES
VMEM_LIMIT = 56 * 1024 * 1024

SLAB_MQ = 0
SLAB_MK = MOBA_WIDTH // LANES
SLAB_MV = 2 * MOBA_WIDTH // LANES
SLAB_RQ = 3 * MOBA_WIDTH // LANES
SLAB_RK = SLAB_RQ + RET_WIDTH // LANES
SLAB_RV = SLAB_RK + RET_WIDTH // LANES
SLAB_RG = SLAB_RV + RET_WIDTH // LANES


def _params(sem, vmem=VMEM_LIMIT):
    return pltpu.CompilerParams(dimension_semantics=sem, vmem_limit_bytes=vmem)


def _dot(a, b):
    return jnp.dot(a, b, preferred_element_type=jnp.float32)


def _dot_nt(a, b):
    return lax.dot_general(a, b, (((1,), (1,)), ((), ())), preferred_element_type=jnp.float32)


def _rmsnorm_kernel(x_ref, g_ref, o_ref, *, transpose):
    x = x_ref[...]
    y = x * lax.rsqrt(jnp.mean(x * x, axis=-1, keepdims=True) + NORM_EPS) * g_ref[...]
    if transpose:
        y = y.T
    o_ref[...] = y.astype(o_ref.dtype)


def _rmsnorm(x, g, *, transpose, tm=256):
    s, d = x.shape
    if transpose:
        out_shape = jax.ShapeDtypeStruct((d, s), jnp.bfloat16)
        out_spec = pl.BlockSpec((d, tm), lambda i: (0, i))
    else:
        out_shape = jax.ShapeDtypeStruct((s, d), jnp.bfloat16)
        out_spec = pl.BlockSpec((tm, d), lambda i: (i, 0))
    return pl.pallas_call(
        functools.partial(_rmsnorm_kernel, transpose=transpose),
        grid=(s // tm,),
        in_specs=[pl.BlockSpec((tm, d), lambda i: (i, 0)),
                  pl.BlockSpec((1, d), lambda i: (0, 0))],
        out_specs=out_spec,
        out_shape=out_shape,
        compiler_params=_params(("parallel",)),
        name="rmsnorm_t" if transpose else "rmsnorm",
    )(x, g.reshape(1, d))


def _inproj_kernel(a_ref, b_ref, o_ref):
    acc = _dot(a_ref[...], b_ref[...])
    for c in range(o_ref.shape[0]):
        o_ref[c] = acc[:, c * LANES:(c + 1) * LANES].astype(o_ref.dtype)


def _inproj(xn, w, *, tm=512, tn=1024):
    s, d = xn.shape
    n = w.shape[1]
    return pl.pallas_call(
        _inproj_kernel,
        grid=(n // tn, s // tm),
        in_specs=[pl.BlockSpec((tm, d), lambda j, i: (i, 0)),
                  pl.BlockSpec((d, tn), lambda j, i: (0, j))],
        out_specs=pl.BlockSpec((tn // LANES, tm, LANES), lambda j, i: (j, i, 0)),
        out_shape=jax.ShapeDtypeStruct((n // LANES, s, LANES), jnp.bfloat16),
        compiler_params=_params(("parallel", "parallel")),
        name="inproj",
    )(xn, w)


def _kmean_kernel(k_ref, o_ref):
    k = k_ref[0].astype(jnp.float32)
    nblk = k.shape[0] // MOBA_BLOCK
    km = jnp.mean(k.reshape(nblk, MOBA_BLOCK, LANES), axis=1)
    pad = jnp.zeros((LANES - nblk, LANES), jnp.float32)
    o_ref[0] = jnp.concatenate([km, pad], axis=0)


def _kmean(proj):
    s = proj.shape[1]
    return pl.pallas_call(
        _kmean_kernel,
        grid=(MOBA_HEADS,),
        in_specs=[pl.BlockSpec((1, s, LANES), lambda h: (SLAB_MK + h, 0, 0))],
        out_specs=pl.BlockSpec((1, LANES, LANES), lambda h: (h, 0, 0)),
        out_shape=jax.ShapeDtypeStruct((MOBA_HEADS, LANES, LANES), jnp.float32),
        compiler_params=_params(("parallel",)),
        name="moba_kmean",
    )(proj)


def _t5_bucket_starts():
    max_exact = REL_BUCKETS // 2
    d = np.arange(0, REL_MAX_DIST + 1)
    nf = np.maximum(d, 1).astype(np.float64)
    large = max_exact + (np.log(nf / max_exact) / math.log(REL_MAX_DIST / max_exact)
                         * (REL_BUCKETS - max_exact)).astype(np.int64)
    large = np.minimum(large, REL_BUCKETS - 1)
    bucket = np.where(d < max_exact, d, large)
    return [int(np.argmax(bucket >= b)) for b in range(REL_BUCKETS)]


_BUCKET_START = _t5_bucket_starts()


def _bias_kernel(rb_ref, o_ref):
    h = pl.program_id(0)
    qi = lax.broadcasted_iota(jnp.int32, (MOBA_BLOCK, MOBA_BLOCK), 0)
    ki = lax.broadcasted_iota(jnp.int32, (MOBA_BLOCK, MOBA_BLOCK), 1)
    for r in range(3):
        d = qi - ki + r * MOBA_BLOCK
        dd = jnp.maximum(d, 0)
        bias = jnp.full((MOBA_BLOCK, MOBA_BLOCK), rb_ref[0, h], jnp.float32)
        for b in range(1, REL_BUCKETS):
            bias = jnp.where(dd >= _BUCKET_START[b], rb_ref[b, h], bias)
        o_ref[0, r] = jnp.where(d >= 0, bias * LOG2E, NEG)


def _bias_tables(rel_bias):
    return pl.pallas_call(
        _bias_kernel,
        grid=(MOBA_HEADS,),
        in_specs=[pl.BlockSpec(memory_space=pltpu.SMEM)],
        out_specs=pl.BlockSpec((1, 3, MOBA_BLOCK, MOBA_BLOCK), lambda h: (h, 0, 0, 0)),
        out_shape=jax.ShapeDtypeStruct((MOBA_HEADS, 3, MOBA_BLOCK, MOBA_BLOCK), jnp.float32),
        compiler_params=_params(("arbitrary",)),
        name="moba_bias",
    )(rel_bias)


def _moba_kernel(q_ref, k_ref, v_ref, km_ref, tbl_ref, o_ref, s_ref):
    qi = pl.program_id(1)
    blk = MOBA_BLOCK
    heads = q_ref.shape[0]
    score_scale = MOBA_HEAD_DIM ** -0.5 * LOG2E
    row = lax.broadcasted_iota(jnp.int32, (LANES, blk), 0)
    rowf = row.astype(jnp.float32)
    valid = row < qi
    lane = lax.broadcasted_iota(jnp.int32, (blk, LANES), 1)

    def augmented_query(hh):
        q = q_ref[hh]
        gate = _dot_nt(km_ref[hh].astype(jnp.bfloat16), q)
        g = jnp.where(valid, gate, NEG)
        sel = row == qi
        for _ in range(MOBA_TOPK):
            m = jnp.max(g, axis=0, keepdims=True)
            idx = jnp.min(jnp.where(g == m, rowf, float(LANES)), axis=0, keepdims=True)
            pick = rowf == idx
            sel = sel | (pick & valid)
            g = jnp.where(pick, -jnp.inf, g)
        mask_bias = jnp.where(sel, 0.0, NEG).T.astype(jnp.bfloat16)
        return jnp.concatenate([q, mask_bias], axis=1)

    q_aug = [augmented_query(hh) for hh in range(heads)]

    n_pairs = (qi + 2) // 2

    def pair_rows(t):
        ja = qi - 2 * t
        jb = ja - 1
        jb_c = jnp.maximum(jb, 0)
        return (ja, jb, pl.ds(pl.multiple_of(ja * blk, blk), blk),
                pl.ds(pl.multiple_of(jb_c * blk, blk), blk))

    def fold(x):
        return [x[:, c * LANES:(c + 1) * LANES] for c in range(2 * blk // LANES)]

    def scores(t, mpart):
        ja, jb, rows_a, rows_b = pair_rows(t)
        hot_a = (lane == ja).astype(jnp.bfloat16)
        hot_b = (lane == jnp.where(jb >= 0, jb, LANES - 1)).astype(jnp.bfloat16)
        rel_a = jnp.minimum(2 * t, 2)
        rel_b = jnp.minimum(2 * t + 1, 2)
        out = []
        for hh in range(heads):
            k_aug = jnp.concatenate(
                [jnp.concatenate([k_ref[hh, rows_a, :], hot_a], axis=1),
                 jnp.concatenate([k_ref[hh, rows_b, :], hot_b], axis=1)], axis=0)
            bias = jnp.concatenate([tbl_ref[hh, rel_a], tbl_ref[hh, rel_b]], axis=1)
            s = _dot_nt(q_aug[hh], k_aug) * score_scale + bias
            s_ref[hh, t] = s
            out.append(functools.reduce(jnp.maximum, fold(s), mpart[hh]))
        return tuple(out)

    mpart = lax.fori_loop(0, n_pairs, scores,
                          tuple(jnp.full((blk, LANES), -jnp.inf, jnp.float32) for _ in range(heads)))
    m_row = [jnp.max(mp, axis=1, keepdims=True) for mp in mpart]

    def values(t, carry):
        _, _, rows_a, rows_b = pair_rows(t)
        out = []
        for hh in range(heads):
            lpart, acc = carry[hh]
            p = jnp.exp2(s_ref[hh, t] - m_row[hh])
            vv = jnp.concatenate([v_ref[hh, rows_a, :], v_ref[hh, rows_b, :]], axis=0)
            out.append((functools.reduce(jnp.add, fold(p), lpart),
                        acc + _dot(p.astype(jnp.bfloat16), vv)))
        return tuple(out)

    zero = jnp.zeros((blk, LANES), jnp.float32)
    fin = lax.fori_loop(0, n_pairs, values, tuple((zero, zero) for _ in range(heads)))
    for hh in range(heads):
        lpart, acc = fin[hh]
        denom = jnp.sum(lpart, axis=1, keepdims=True)
        o_ref[:, hh * LANES:(hh + 1) * LANES] = (acc / denom).astype(o_ref.dtype)


def _moba(proj, kmean, tables, *, heads=MOBA_HEADS_PER_STEP):
    s = proj.shape[1]
    nq = s // MOBA_BLOCK
    assert nq < LANES - 1 and MOBA_HEADS % heads == 0
    slab = lambda base: (lambda h, i: (base // heads + h, 0, 0))
    return pl.pallas_call(
        _moba_kernel,
        grid=(MOBA_HEADS // heads, nq),
        in_specs=[pl.BlockSpec((heads, MOBA_BLOCK, LANES), lambda h, i: (SLAB_MQ // heads + h, i, 0)),
                  pl.BlockSpec((heads, s, LANES), slab(SLAB_MK)),
                  pl.BlockSpec((heads, s, LANES), slab(SLAB_MV)),
                  pl.BlockSpec((heads, LANES, LANES), slab(0)),
                  pl.BlockSpec((heads, 3, MOBA_BLOCK, MOBA_BLOCK), lambda h, i: (h, 0, 0, 0))],
        out_specs=pl.BlockSpec((MOBA_BLOCK, heads * LANES), lambda h, i: (i, h)),
        out_shape=jax.ShapeDtypeStruct((s, MOBA_WIDTH), jnp.bfloat16),
        scratch_shapes=[pltpu.VMEM((heads, (nq + 1) // 2, MOBA_BLOCK, 2 * MOBA_BLOCK), jnp.float32)],
        compiler_params=_params(("parallel", "parallel")),
        name="moba_attn",
    )(proj, proj, proj, kmean, tables)


def _retention_kernel(lg_ref, q_ref, k_ref, v_ref, g_ref, cos_ref, sin_ref, o_ref,
                      state_ref, decay_ref):
    h = pl.program_id(0)
    c = pl.program_id(1)
    cs = RET_CHUNK
    lg = lg_ref[h]

    @pl.when(c == 0)
    def _():
        state_ref[...] = jnp.zeros_like(state_ref)
        i = lax.broadcasted_iota(jnp.int32, (cs, cs), 0)
        j = lax.broadcasted_iota(jnp.int32, (cs, cs), 1)
        diff = (i - j).astype(jnp.float32)
        decay_ref[...] = jnp.where(diff >= 0, jnp.exp(lg * jnp.maximum(diff, 0.0)), 0.0)

    cos = cos_ref[...]
    sin = sin_ref[...]

    def rope(ref):
        x1 = ref[0].astype(jnp.float32)
        x2 = ref[1].astype(jnp.float32)
        return jnp.concatenate([x1 * cos - x2 * sin, x1 * sin + x2 * cos], axis=1)

    rq = rope(q_ref)
    rk = rope(k_ref) * (RET_HEAD_DIM ** -0.5)
    v = jnp.concatenate([v_ref[0], v_ref[1]], axis=1)
    row = lax.broadcasted_iota(jnp.int32, (cs, 1), 0).astype(jnp.float32)
    q_decay = jnp.exp(lg * (row + 1.0))
    k_decay = jnp.exp(lg * (cs - 1.0 - row))

    state = state_ref[...]
    inner = _dot_nt(rq.astype(jnp.bfloat16), rk.astype(jnp.bfloat16)) * decay_ref[...]
    o = (_dot(inner.astype(jnp.bfloat16), v)
         + _dot((rq * q_decay).astype(jnp.bfloat16), state.astype(jnp.bfloat16)))
    kd_t = (rk * k_decay).T.astype(jnp.bfloat16)
    state_ref[...] = state * jnp.exp(lg * cs) + _dot(kd_t, v)

    mu = jnp.mean(o, axis=-1, keepdims=True)
    ctr = o - mu
    var = jnp.mean(ctr * ctr, axis=-1, keepdims=True)
    y = ctr * lax.rsqrt(var + GN_EPS)
    gate = jnp.concatenate([g_ref[0], g_ref[1]], axis=1).astype(jnp.float32)
    o_ref[...] = (gate * jax.nn.sigmoid(gate) * y).astype(o_ref.dtype)


def _retention(proj, log_g, cos, sin):
    s = proj.shape[1]
    cs = RET_CHUNK
    pair = lambda base: pl.BlockSpec((2, cs, LANES), lambda h, c: (base // 2 + h, c, 0))
    return pl.pallas_call(
        _retention_kernel,
        grid=(RET_HEADS, s // cs),
        in_specs=[pl.BlockSpec(memory_space=pltpu.SMEM),
                  pair(SLAB_RQ), pair(SLAB_RK), pair(SLAB_RV), pair(SLAB_RG),
                  pl.BlockSpec((cs, LANES), lambda h, c: (c, 0)),
                  pl.BlockSpec((cs, LANES), lambda h, c: (c, 0))],
        out_specs=pl.BlockSpec((cs, RET_HEAD_DIM), lambda h, c: (c, h)),
        out_shape=jax.ShapeDtypeStruct((s, RET_WIDTH), jnp.bfloat16),
        scratch_shapes=[pltpu.VMEM((RET_HEAD_DIM, RET_HEAD_DIM), jnp.float32),
                        pltpu.VMEM((cs, cs), jnp.float32)],
        compiler_params=_params(("parallel", "arbitrary")),
        name="retention",
    )(log_g, proj, proj, proj, proj, cos, sin)


def _outproj_kernel(a_ref, r_ref, wa_ref, wr_ref, x_ref, o_ref):
    o_ref[...] = x_ref[...] + _dot(a_ref[...], wa_ref[...]) + _dot(r_ref[...], wr_ref[...])


def _outproj(a_out, r_out, w_out, x, *, tm=512, tn=1024):
    s, d = x.shape
    ka = a_out.shape[1]
    kr = r_out.shape[1]
    return pl.pallas_call(
        _outproj_kernel,
        grid=(d // tn, s // tm),
        in_specs=[pl.BlockSpec((tm, ka), lambda j, i: (i, 0)),
                  pl.BlockSpec((tm, kr), lambda j, i: (i, 0)),
                  pl.BlockSpec((ka, tn), lambda j, i: (0, j)),
                  pl.BlockSpec((kr, tn), lambda j, i: (ka // kr, j)),
                  pl.BlockSpec((tm, tn), lambda j, i: (i, j))],
        out_specs=pl.BlockSpec((tm, tn), lambda j, i: (i, j)),
        out_shape=jax.ShapeDtypeStruct((s, d), jnp.float32),
        compiler_params=_params(("parallel", "parallel")),
        name="outproj",
    )(a_out, r_out, w_out, w_out, x)


def _knockout_topk(s, k, on_value):
    r = s.shape[0]
    rid = lax.broadcasted_iota(jnp.int32, s.shape, 0).astype(jnp.float32)
    for i in range(k):
        m = jnp.max(s, axis=0, keepdims=True)
        on_value(i, m)
        if i + 1 < k:
            idx = jnp.min(jnp.where(s == m, rid, float(r)), axis=0, keepdims=True)
            s = jnp.where(rid == idx, -jnp.inf, s)


def _peer_prep_kernel(xt_ref, wq_ref, keys_ref, s0_ref, s1_ref, thr_ref, nrm_ref, a_ref, b_ref):
    nk = PEER_NKEYS
    qt = _dot(wq_ref[...], xt_ref[...]).astype(jnp.bfloat16)
    s0 = _dot(keys_ref[0], qt[:nk])
    s1 = _dot(keys_ref[1], qt[nk:])
    s0_ref[0] = s0
    s1_ref[0] = s1

    def put(ref):
        def f(i, m):
            ref[i:i + 1, :] = m
        return f

    _knockout_topk(s0, PEER_TOPK, put(a_ref))
    _knockout_topk(s1, PEER_TOPK, put(b_ref))
    a = a_ref[...]
    b = b_ref[...]
    half = PEER_TOPK // 2
    cands = [a[0:1] + b]
    for i in range(1, half):
        cands.append(a[i:i + 1] + b[0:half])
    cands.append(a[half:] + b[0:1])
    cand = jnp.concatenate(cands, axis=0)

    top = a[0:1] + b[0:1]
    stats = {"z": jnp.zeros_like(top), "thr": top}

    def acc(i, m):
        stats["z"] = stats["z"] + jnp.exp(m - top)
        stats["thr"] = m

    _knockout_topk(cand, PEER_TOPK, acc)
    thr_ref[0] = stats["thr"]
    nrm_ref[0] = top + jnp.log(stats["z"])


def _peer_prep(xnt, wqt, keys, *, tt=512):
    d, t = xnt.shape
    nk = PEER_NKEYS
    sc_spec = pl.BlockSpec((1, nk, tt), lambda i, h: (h, 0, i))
    row_spec = pl.BlockSpec((1, 1, tt), lambda i, h: (h, 0, i))
    sc_shape = jax.ShapeDtypeStruct((PEER_HEADS, nk, t), jnp.float32)
    row_shape = jax.ShapeDtypeStruct((PEER_HEADS, 1, t), jnp.float32)
    return pl.pallas_call(
        _peer_prep_kernel,
        grid=(t // tt, PEER_HEADS),
        in_specs=[pl.BlockSpec((d, tt), lambda i, h: (0, i)),
                  pl.BlockSpec((PEER_QDIM, d), lambda i, h: (h, 0)),
                  pl.BlockSpec((2, nk, nk), lambda i, h: (h, 0, 0))],
        out_specs=[sc_spec, sc_spec, row_spec, row_spec],
        out_shape=[sc_shape, sc_shape, row_shape, row_shape],
        scratch_shapes=[pltpu.VMEM((PEER_TOPK, tt), jnp.float32),
                        pltpu.VMEM((PEER_TOPK, tt), jnp.float32)],
        compiler_params=_params(("parallel", "arbitrary")),
        name="peer_prep",
    )(xnt, wqt, keys)


def _peer_kernel(xt_ref, u_ref, vt_ref, s0_ref, s1_ref, thr_ref, nrm_ref, o_ref):
    et = pl.program_id(1)
    nk = PEER_NKEYS
    rows = u_ref.shape[0] // nk
    pre = _dot(u_ref[...], xt_ref[...])
    act = 0.5 * pre * (1.0 + lax.erf(pre * math.sqrt(0.5)))
    pieces = []
    for r in range(rows):
        i = et * rows + r
        w = jnp.zeros((nk, xt_ref.shape[1]), jnp.float32)
        for h in range(PEER_HEADS):
            z = s0_ref[h, pl.ds(i, 1), :] + s1_ref[h]
            w = w + jnp.where(z >= thr_ref[h], jnp.exp(z - nrm_ref[h]), 0.0)
        pieces.append((act[r * nk:(r + 1) * nk] * w).astype(jnp.bfloat16))
    contrib = _dot(vt_ref[...], jnp.concatenate(pieces, axis=0))

    @pl.when(et == 0)
    def _():
        o_ref[...] = contrib

    @pl.when(et > 0)
    def _():
        o_ref[...] += contrib


def _peer(xnt, u, vt, s0, s1, thr, nrm, *, tt=512, te=512):
    d, t = xnt.shape
    e = u.shape[0]
    nk = PEER_NKEYS
    full = pl.BlockSpec((PEER_HEADS, nk, tt), lambda i, j: (0, 0, i))
    rowv = pl.BlockSpec((PEER_HEADS, 1, tt), lambda i, j: (0, 0, i))
    return pl.pallas_call(
        _peer_kernel,
        grid=(t // tt, e // te),
        in_specs=[pl.BlockSpec((d, tt), lambda i, j: (0, i)),
                  pl.BlockSpec((te, d), lambda i, j: (j, 0)),
                  pl.BlockSpec((d, te), lambda i, j: (0, j)),
                  full, full, rowv, rowv],
        out_specs=pl.BlockSpec((d, tt), lambda i, j: (0, i)),
        out_shape=jax.ShapeDtypeStruct((d, t), jnp.float32),
        compiler_params=_params(("parallel", "arbitrary")),
        name="peer_experts",
    )(xnt, u, vt, s0, s1, thr, nrm)


def _final_kernel(h_ref, pt_ref, g_ref, o_ref):
    x = h_ref[...] + pt_ref[...].T
    o_ref[...] = x * lax.rsqrt(jnp.mean(x * x, axis=-1, keepdims=True) + NORM_EPS) * g_ref[...]


def _final(h, peer_t, g, *, tm=256):
    s, d = h.shape
    return pl.pallas_call(
        _final_kernel,
        grid=(s // tm,),
        in_specs=[pl.BlockSpec((tm, d), lambda i: (i, 0)),
                  pl.BlockSpec((d, tm), lambda i: (0, i)),
                  pl.BlockSpec((1, d), lambda i: (0, 0))],
        out_specs=pl.BlockSpec((tm, d), lambda i: (i, 0)),
        out_shape=jax.ShapeDtypeStruct((s, d), jnp.float32),
        compiler_params=_params(("parallel",)),
        name="final_norm",
    )(h, peer_t, g.reshape(1, d))


def kernel(x, norm_mix_g, w_in, w_out, rel_bias, norm_ffn_g, peer_w_q, peer_sub_keys,
           peer_u, peer_v, norm_final_g):
    b, s, d = x.shape
    assert b == 1 and d == D_MODEL and s % RET_CHUNK == 0 and norm_mix_g.shape[0] == 1
    bf16 = jnp.bfloat16
    h = x.reshape(s, d)

    half = RET_HEAD_DIM // 2
    inv = ROPE_BASE ** (-jnp.arange(half, dtype=jnp.float32) / half)
    ang = jnp.arange(s, dtype=jnp.int32).astype(jnp.float32)[:, None] * inv[None, :]
    cos, sin = jnp.cos(ang), jnp.sin(ang)
    log_g = jnp.log1p(-(2.0 ** (-5.0 - jnp.arange(RET_HEADS, dtype=jnp.float32))))

    xn = _rmsnorm(h, norm_mix_g[0], transpose=False)
    proj = _inproj(xn, w_in[0].astype(bf16))
    a_out = _moba(proj, _kmean(proj), _bias_tables(rel_bias))
    r_out = _retention(proj, log_g, cos, sin)
    h = _outproj(a_out, r_out, w_out[0].astype(bf16), h)

    xnt = _rmsnorm(h, norm_ffn_g[0], transpose=True)
    keys = peer_sub_keys[0].astype(bf16).reshape(PEER_HEADS * 2, PEER_NKEYS, PEER_QDIM // 2)
    s0, s1, thr, nrm = _peer_prep(xnt, peer_w_q[0].T.astype(bf16), keys)
    peer_t = _peer(xnt, peer_u[0].astype(bf16), peer_v[0].T.astype(bf16), s0, s1, thr, nrm)
    y = _final(h, peer_t, norm_final_g)
    return y.reshape(b, s, d)
```

```python
import functools
import math

import numpy as np
import jax
import jax.numpy as jnp
from jax import lax
from jax.experimental import pallas as pl
from jax.experimental.pallas import tpu as pltpu

D_MODEL = 4096
MOBA_HEADS = 16
MOBA_HEAD_DIM = 128
MOBA_WIDTH = MOBA_HEADS * MOBA_HEAD_DIM
MOBA_BLOCK = 256
MOBA_TOPK = 3
RET_HEADS = 8
RET_HEAD_DIM = 256
RET_WIDTH = RET_HEADS * RET_HEAD_DIM
RET_CHUNK = 512
ROPE_BASE = 10000.0
REL_BUCKETS = 32
REL_MAX_DIST = 128
IN_WIDTH = 3 * MOBA_WIDTH + 4 * RET_WIDTH
PEER_HEADS = 8
PEER_NKEYS = 128
PEER_EXPERTS = PEER_NKEYS * PEER_NKEYS
PEER_QDIM = 256
PEER_TOPK = 16
NORM_EPS = 1e-6
GN_EPS = 1e-6
NEG = -1e30

LOG2E = math.log2(math.e)
MOBA_HEADS_PER_STEP = 2
LANES = 128
N_SLABS = IN_WIDTH // LANES
VMEM_LIMIT = 56 * 1024 * 1024

SLAB_MQ = 0
SLAB_MK = MOBA_WIDTH // LANES
SLAB_MV = 2 * MOBA_WIDTH // LANES
SLAB_RQ = 3 * MOBA_WIDTH // LANES
SLAB_RK = SLAB_RQ + RET_WIDTH // LANES
SLAB_RV = SLAB_RK + RET_WIDTH // LANES
SLAB_RG = SLAB_RV + RET_WIDTH // LANES


def _params(sem, vmem=VMEM_LIMIT):
    return pltpu.CompilerParams(dimension_semantics=sem, vmem_limit_bytes=vmem)


def _dot(a, b):
    return jnp.dot(a, b, preferred_element_type=jnp.float32)


def _dot_nt(a, b):
    return lax.dot_general(a, b, (((1,), (1,)), ((), ())), preferred_element_type=jnp.float32)


def _rmsnorm_kernel(x_ref, g_ref, o_ref, *, transpose):
    x = x_ref[...]
    y = x * lax.rsqrt(jnp.mean(x * x, axis=-1, keepdims=True) + NORM_EPS) * g_ref[...]
    if transpose:
        y = y.T
    o_ref[...] = y.astype(o_ref.dtype)


def _rmsnorm(x, g, *, transpose, tm=256):
    s, d = x.shape
    if transpose:
        out_shape = jax.ShapeDtypeStruct((d, s), jnp.bfloat16)
        out_spec = pl.BlockSpec((d, tm), lambda i: (0, i))
    else:
        out_shape = jax.ShapeDtypeStruct((s, d), jnp.bfloat16)
        out_spec = pl.BlockSpec((tm, d), lambda i: (i, 0))
    return pl.pallas_call(
        functools.partial(_rmsnorm_kernel, transpose=transpose),
        grid=(s // tm,),
        in_specs=[pl.BlockSpec((tm, d), lambda i: (i, 0)),
                  pl.BlockSpec((1, d), lambda i: (0, 0))],
        out_specs=out_spec,
        out_shape=out_shape,
        compiler_params=_params(("parallel",)),
        name="rmsnorm_t" if transpose else "rmsnorm",
    )(x, g.reshape(1, d))


def _inproj_kernel(a_ref, b_ref, o_ref):
    acc = _dot(a_ref[...], b_ref[...])
    for c in range(o_ref.shape[0]):
        o_ref[c] = acc[:, c * LANES:(c + 1) * LANES].astype(o_ref.dtype)


def _inproj(xn, w, *, tm=512, tn=1024):
    s, d = xn.shape
    n = w.shape[1]
    return pl.pallas_call(
        _inproj_kernel,
        grid=(n // tn, s // tm),
        in_specs=[pl.BlockSpec((tm, d), lambda j, i: (i, 0)),
                  pl.BlockSpec((d, tn), lambda j, i: (0, j))],
        out_specs=pl.BlockSpec((tn // LANES, tm, LANES), lambda j, i: (j, i, 0)),
        out_shape=jax.ShapeDtypeStruct((n // LANES, s, LANES), jnp.bfloat16),
        compiler_params=_params(("parallel", "parallel")),
        name="inproj",
    )(xn, w)


def _kmean_kernel(k_ref, o_ref):
    k = k_ref[0].astype(jnp.float32)
    nblk = k.shape[0] // MOBA_BLOCK
    km = jnp.mean(k.reshape(nblk, MOBA_BLOCK, LANES), axis=1)
    pad = jnp.zeros((LANES - nblk, LANES), jnp.float32)
    o_ref[0] = jnp.concatenate([km, pad], axis=0)


def _kmean(proj):
    s = proj.shape[1]
    return pl.pallas_call(
        _kmean_kernel,
        grid=(MOBA_HEADS,),
        in_specs=[pl.BlockSpec((1, s, LANES), lambda h: (SLAB_MK + h, 0, 0))],
        out_specs=pl.BlockSpec((1, LANES, LANES), lambda h: (h, 0, 0)),
        out_shape=jax.ShapeDtypeStruct((MOBA_HEADS, LANES, LANES), jnp.float32),
        compiler_params=_params(("parallel",)),
        name="moba_kmean",
    )(proj)


def _t5_bucket_starts():
    max_exact = REL_BUCKETS // 2
    d = np.arange(0, REL_MAX_DIST + 1)
    nf = np.maximum(d, 1).astype(np.float64)
    large = max_exact + (np.log(nf / max_exact) / math.log(REL_MAX_DIST / max_exact)
                         * (REL_BUCKETS - max_exact)).astype(np.int64)
    large = np.minimum(large, REL_BUCKETS - 1)
    bucket = np.where(d < max_exact, d, large)
    return [int(np.argmax(bucket >= b)) for b in range(REL_BUCKETS)]


_BUCKET_START = _t5_bucket_starts()


def _bias_kernel(rb_ref, o_ref):
    h = pl.program_id(0)
    qi = lax.broadcasted_iota(jnp.int32, (MOBA_BLOCK, MOBA_BLOCK), 0)
    ki = lax.broadcasted_iota(jnp.int32, (MOBA_BLOCK, MOBA_BLOCK), 1)
    for r in range(3):
        d = qi - ki + r * MOBA_BLOCK
        dd = jnp.maximum(d, 0)
        bias = jnp.full((MOBA_BLOCK, MOBA_BLOCK), rb_ref[0, h], jnp.float32)
        for b in range(1, REL_BUCKETS):
            bias = jnp.where(dd >= _BUCKET_START[b], rb_ref[b, h], bias)
        o_ref[0, r] = jnp.where(d >= 0, bias * LOG2E, NEG)


def _bias_tables(rel_bias):
    return pl.pallas_call(
        _bias_kernel,
        grid=(MOBA_HEADS,),
        in_specs=[pl.BlockSpec(memory_space=pltpu.SMEM)],
        out_specs=pl.BlockSpec((1, 3, MOBA_BLOCK, MOBA_BLOCK), lambda h: (h, 0, 0, 0)),
        out_shape=jax.ShapeDtypeStruct((MOBA_HEADS, 3, MOBA_BLOCK, MOBA_BLOCK), jnp.float32),
        compiler_params=_params(("arbitrary",)),
        name="moba_bias",
    )(rel_bias)


def _moba_kernel(q_ref, k_ref, v_ref, km_ref, tbl_ref, o_ref, s_ref):
    qi = pl.program_id(1)
    blk = MOBA_BLOCK
    heads = q_ref.shape[0]
    score_scale = MOBA_HEAD_DIM ** -0.5 * LOG2E
    row = lax.broadcasted_iota(jnp.int32, (LANES, blk), 0)
    rowf = row.astype(jnp.float32)
    valid = row < qi
    lane = lax.broadcasted_iota(jnp.int32, (blk, LANES), 1)

    def augmented_query(hh):
        q = q_ref[hh]
        gate = _dot_nt(km_ref[hh].astype(jnp.bfloat16), q)
        g = jnp.where(valid, gate, NEG)
        sel = row == qi
        for _ in range(MOBA_TOPK):
            m = jnp.max(g, axis=0, keepdims=True)
            idx = jnp.min(jnp.where(g == m, rowf, float(LANES)), axis=0, keepdims=True)
            pick = rowf == idx
            sel = sel | (pick & valid)
            g = jnp.where(pick, -jnp.inf, g)
        mask_bias = jnp.where(sel, 0.0, NEG).T.astype(jnp.bfloat16)
        return jnp.concatenate([q, mask_bias], axis=1)

    q_aug = [augmented_query(hh) for hh in range(heads)]

    n_pairs = (qi + 2) // 2

    def pair_rows(t):
        ja = qi - 2 * t
        jb = ja - 1
        jb_c = jnp.maximum(jb, 0)
        return (ja, jb, pl.ds(pl.multiple_of(ja * blk, blk), blk),
                pl.ds(pl.multiple_of(jb_c * blk, blk), blk))

    def fold(x):
        return [x[:, c * LANES:(c + 1) * LANES] for c in range(2 * blk // LANES)]

    def scores(t, mpart):
        ja, jb, rows_a, rows_b = pair_rows(t)
        hot_a = (lane == ja).astype(jnp.bfloat16)
        hot_b = (lane == jnp.where(jb >= 0, jb, LANES - 1)).astype(jnp.bfloat16)
        rel_a = jnp.minimum(2 * t, 2)
        rel_b = jnp.minimum(2 * t + 1, 2)
        out = []
        for hh in range(heads):
            k_aug = jnp.concatenate(
                [jnp.concatenate([k_ref[hh, rows_a, :], hot_a], axis=1),
                 jnp.concatenate([k_ref[hh, rows_b, :], hot_b], axis=1)], axis=0)
            bias = jnp.concatenate([tbl_ref[hh, rel_a], tbl_ref[hh, rel_b]], axis=1)
            s = _dot_nt(q_aug[hh], k_aug) * score_scale + bias
            s_ref[hh, t] = s
            out.append(functools.reduce(jnp.maximum, fold(s), mpart[hh]))
        return tuple(out)

    mpart = lax.fori_loop(0, n_pairs, scores,
                          tuple(jnp.full((blk, LANES), -jnp.inf, jnp.float32) for _ in range(heads)))
    m_row = [jnp.max(mp, axis=1, keepdims=True) for mp in mpart]

    def values(t, carry):
        _, _, rows_a, rows_b = pair_rows(t)
        out = []
        for hh in range(heads):
            lpart, acc = carry[hh]
            p = jnp.exp2(s_ref[hh, t] - m_row[hh])
            vv = jnp.concatenate([v_ref[hh, rows_a, :], v_ref[hh, rows_b, :]], axis=0)
            out.append((functools.reduce(jnp.add, fold(p), lpart),
                        acc + _dot(p.astype(jnp.bfloat16), vv)))
        return tuple(out)

    zero = jnp.zeros((blk, LANES), jnp.float32)
    fin = lax.fori_loop(0, n_pairs, values, tuple((zero, zero) for _ in range(heads)))
    for hh in range(heads):
        lpart, acc = fin[hh]
        denom = jnp.sum(lpart, axis=1, keepdims=True)
        o_ref[:, hh * LANES:(hh + 1) * LANES] = (acc / denom).astype(o_ref.dtype)


def _moba(proj, kmean, tables, *, heads=MOBA_HEADS_PER_STEP):
    s = proj.shape[1]
    nq = s // MOBA_BLOCK
    assert nq < LANES - 1 and MOBA_HEADS % heads == 0
    slab = lambda base: (lambda h, i: (base // heads + h, 0, 0))
    return pl.pallas_call(
        _moba_kernel,
        grid=(MOBA_HEADS // heads, nq),
        in_specs=[pl.BlockSpec((heads, MOBA_BLOCK, LANES), lambda h, i: (SLAB_MQ // heads + h, i, 0)),
                  pl.BlockSpec((heads, s, LANES), slab(SLAB_MK)),
                  pl.BlockSpec((heads, s, LANES), slab(SLAB_MV)),
                  pl.BlockSpec((heads, LANES, LANES), slab(0)),
                  pl.BlockSpec((heads, 3, MOBA_BLOCK, MOBA_BLOCK), lambda h, i: (h, 0, 0, 0))],
        out_specs=pl.BlockSpec((MOBA_BLOCK, heads * LANES), lambda h, i: (i, h)),
        out_shape=jax.ShapeDtypeStruct((s, MOBA_WIDTH), jnp.bfloat16),
        scratch_shapes=[pltpu.VMEM((heads, (nq + 1) // 2, MOBA_BLOCK, 2 * MOBA_BLOCK), jnp.float32)],
        compiler_params=_params(("parallel", "parallel")),
        name="moba_attn",
    )(proj, proj, proj, kmean, tables)


def _retention_kernel(lg_ref, q_ref, k_ref, v_ref, g_ref, cos_ref, sin_ref, o_ref,
                      state_ref, decay_ref):
    h = pl.program_id(0)
    c = pl.program_id(1)
    cs = RET_CHUNK
    lg = lg_ref[h]

    @pl.when(c == 0)
    def _():
        state_ref[...] = jnp.zeros_like(state_ref)
        i = lax.broadcasted_iota(jnp.int32, (cs, cs), 0)
        j = lax.broadcasted_iota(jnp.int32, (cs, cs), 1)
        diff = (i - j).astype(jnp.float32)
        decay_ref[...] = jnp.where(diff >= 0, jnp.exp(lg * jnp.maximum(diff, 0.0)), 0.0)

    cos = cos_ref[...]
    sin = sin_ref[...]

    def rope(ref):
        x1 = ref[0].astype(jnp.float32)
        x2 = ref[1].astype(jnp.float32)
        return jnp.concatenate([x1 * cos - x2 * sin, x1 * sin + x2 * cos], axis=1)

    rq = rope(q_ref)
    rk = rope(k_ref) * (RET_HEAD_DIM ** -0.5)
    v = jnp.concatenate([v_ref[0], v_ref[1]], axis=1)
    row = lax.broadcasted_iota(jnp.int32, (cs, 1), 0).astype(jnp.float32)
    q_decay = jnp.exp(lg * (row + 1.0))
    k_decay = jnp.exp(lg * (cs - 1.0 - row))

    state = state_ref[...]
    inner = _dot_nt(rq.astype(jnp.bfloat16), rk.astype(jnp.bfloat16)) * decay_ref[...]
    o = (_dot(inner.astype(jnp.bfloat16), v)
         + _dot((rq * q_decay).astype(jnp.bfloat16), state.astype(jnp.bfloat16)))
    kd_t = (rk * k_decay).T.astype(jnp.bfloat16)
    state_ref[...] = state * jnp.exp(lg * cs) + _dot(kd_t, v)

    mu = jnp.mean(o, axis=-1, keepdims=True)
    ctr = o - mu
    var = jnp.mean(ctr * ctr, axis=-1, keepdims=True)
    y = ctr * lax.rsqrt(var + GN_EPS)
    gate = jnp.concatenate([g_ref[0], g_ref[1]], axis=1).astype(jnp.float32)
    o_ref[...] = (gate * jax.nn.sigmoid(gate) * y).astype(o_ref.dtype)


def _retention(proj, log_g, cos, sin):
    s = proj.shape[1]
    cs = RET_CHUNK
    pair = lambda base: pl.BlockSpec((2, cs, LANES), lambda h, c: (base // 2 + h, c, 0))
    return pl.pallas_call(
        _retention_kernel,
        grid=(RET_HEADS, s // cs),
        in_specs=[pl.BlockSpec(memory_space=pltpu.SMEM),
                  pair(SLAB_RQ), pair(SLAB_RK), pair(SLAB_RV), pair(SLAB_RG),
                  pl.BlockSpec((cs, LANES), lambda h, c: (c, 0)),
                  pl.BlockSpec((cs, LANES), lambda h, c: (c, 0))],
        out_specs=pl.BlockSpec((cs, RET_HEAD_DIM), lambda h, c: (c, h)),
        out_shape=jax.ShapeDtypeStruct((s, RET_WIDTH), jnp.bfloat16),
        scratch_shapes=[pltpu.VMEM((RET_HEAD_DIM, RET_HEAD_DIM), jnp.float32),
                        pltpu.VMEM((cs, cs), jnp.float32)],
        compiler_params=_params(("parallel", "arbitrary")),
        name="retention",
    )(log_g, proj, proj, proj, proj, cos, sin)


def _outproj_kernel(a_ref, r_ref, wa_ref, wr_ref, x_ref, o_ref):
    o_ref[...] = x_ref[...] + _dot(a_ref[...], wa_ref[...]) + _dot(r_ref[...], wr_ref[...])


def _outproj(a_out, r_out, w_out, x, *, tm=512, tn=1024):
    s, d = x.shape
    ka = a_out.shape[1]
    kr = r_out.shape[1]
    return pl.pallas_call(
        _outproj_kernel,
        grid=(d // tn, s // tm),
        in_specs=[pl.BlockSpec((tm, ka), lambda j, i: (i, 0)),
                  pl.BlockSpec((tm, kr), lambda j, i: (i, 0)),
                  pl.BlockSpec((ka, tn), lambda j, i: (0, j)),
                  pl.BlockSpec((kr, tn), lambda j, i: (ka // kr, j)),
                  pl.BlockSpec((tm, tn), lambda j, i: (i, j))],
        out_specs=pl.BlockSpec((tm, tn), lambda j, i: (i, j)),
        out_shape=jax.ShapeDtypeStruct((s, d), jnp.float32),
        compiler_params=_params(("parallel", "parallel")),
        name="outproj",
    )(a_out, r_out, w_out, w_out, x)


def _knockout_topk(s, k, on_value):
    r = s.shape[0]
    rid = lax.broadcasted_iota(jnp.int32, s.shape, 0).astype(jnp.float32)
    for i in range(k):
        m = jnp.max(s, axis=0, keepdims=True)
        on_value(i, m)
        if i + 1 < k:
            idx = jnp.min(jnp.where(s == m, rid, float(r)), axis=0, keepdims=True)
            s = jnp.where(rid == idx, -jnp.inf, s)


def _peer_prep_kernel(xt_ref, wq_ref, keys_ref, s0_ref, s1_ref, thr_ref, nrm_ref, a_ref, b_ref):
    nk = PEER_NKEYS
    qt = _dot(wq_ref[...], xt_ref[...]).astype(jnp.bfloat16)
    s0 = _dot(keys_ref[0], qt[:nk])
    s1 = _dot(keys_ref[1], qt[nk:])
    s0_ref[0] = s0
    s1_ref[0] = s1

    def put(ref):
        def f(i, m):
            ref[i:i + 1, :] = m
        return f

    _knockout_topk(s0, PEER_TOPK, put(a_ref))
    _knockout_topk(s1, PEER_TOPK, put(b_ref))
    a = a_ref[...]
    b = b_ref[...]
    half = PEER_TOPK // 2
    cands = [a[0:1] + b]
    for i in range(1, half):
        cands.append(a[i:i + 1] + b[0:half])
    cands.append(a[half:] + b[0:1])
    cand = jnp.concatenate(cands, axis=0)

    top = a[0:1] + b[0:1]
    stats = {"z": jnp.zeros_like(top), "thr": top}

    def acc(i, m):
        stats["z"] = stats["z"] + jnp.exp(m - top)
        stats["thr"] = m

    _knockout_topk(cand, PEER_TOPK, acc)
    thr_ref[0] = stats["thr"]
    nrm_ref[0] = top + jnp.log(stats["z"])


def _peer_prep(xnt, wqt, keys, *, tt=512):
    d, t = xnt.shape
    nk = PEER_NKEYS
    sc_spec = pl.BlockSpec((1, nk, tt), lambda i, h: (h, 0, i))
    row_spec = pl.BlockSpec((1, 1, tt), lambda i, h: (h, 0, i))
    sc_shape = jax.ShapeDtypeStruct((PEER_HEADS, nk, t), jnp.float32)
    row_shape = jax.ShapeDtypeStruct((PEER_HEADS, 1, t), jnp.float32)
    return pl.pallas_call(
        _peer_prep_kernel,
        grid=(t // tt, PEER_HEADS),
        in_specs=[pl.BlockSpec((d, tt), lambda i, h: (0, i)),
                  pl.BlockSpec((PEER_QDIM, d), lambda i, h: (h, 0)),
                  pl.BlockSpec((2, nk, nk), lambda i, h: (h, 0, 0))],
        out_specs=[sc_spec, sc_spec, row_spec, row_spec],
        out_shape=[sc_shape, sc_shape, row_shape, row_shape],
        scratch_shapes=[pltpu.VMEM((PEER_TOPK, tt), jnp.float32),
                        pltpu.VMEM((PEER_TOPK, tt), jnp.float32)],
        compiler_params=_params(("parallel", "arbitrary")),
        name="peer_prep",
    )(xnt, wqt, keys)


def _peer_kernel(xt_ref, u_ref, vt_ref, s0_ref, s1_ref, thr_ref, nrm_ref, o_ref,
                 pre0_ref, pre1_ref, gw0_ref, gw1_ref, *, n_tiles):
    k = pl.program_id(1)
    nk = PEER_NKEYS
    te = pre0_ref.shape[0]
    rows = te // nk

    @pl.when(k == 0)
    def _():
        o_ref[...] = jnp.zeros_like(o_ref)
        pre1_ref[...] = jnp.zeros_like(pre1_ref)
        gw0_ref[...] = jnp.zeros_like(gw0_ref)

    def half_step(half, gate_tile, gw_in, pre_out, pre_in, gw_out):
        cols = slice(half * te, (half + 1) * te)
        o_ref[...] += _dot(vt_ref[:, cols], gw_in[...])
        pre_out[...] = _dot(u_ref[cols, :], xt_ref[...])
        for r in range(rows):
            rr = slice(r * nk, (r + 1) * nk)
            pre = pre_in[rr, :]
            act = 0.5 * pre * (1.0 + lax.erf(pre * math.sqrt(0.5)))
            w = jnp.zeros_like(pre)
            for h in range(PEER_HEADS):
                z = s0_ref[h, pl.ds(gate_tile * rows + r, 1), :] + s1_ref[h]
                w = w + jnp.where(z >= thr_ref[h], jnp.exp(z - nrm_ref[h]), 0.0)
            gw_out[rr, :] = (act * w).astype(gw_out.dtype)

    last = n_tiles - 1
    half_step(0, jnp.clip(2 * k - 1, 0, last), gw0_ref, pre0_ref, pre1_ref, gw1_ref)
    half_step(1, jnp.minimum(2 * k, last), gw1_ref, pre1_ref, pre0_ref, gw0_ref)


def _peer(xnt, u, vt, s0, s1, thr, nrm, *, tt=512, te=256):
    d, t = xnt.shape
    e = u.shape[0]
    nk = PEER_NKEYS
    n_tiles = e // te
    n_steps = n_tiles // 2 + 1
    full = pl.BlockSpec((PEER_HEADS, nk, tt), lambda i, k: (0, 0, i))
    rowv = pl.BlockSpec((PEER_HEADS, 1, tt), lambda i, k: (0, 0, i))
    return pl.pallas_call(
        functools.partial(_peer_kernel, n_tiles=n_tiles),
        grid=(t // tt, n_steps),
        in_specs=[pl.BlockSpec((d, tt), lambda i, k: (0, i)),
                  pl.BlockSpec((2 * te, d), lambda i, k: (jnp.minimum(k, n_steps - 2), 0)),
                  pl.BlockSpec((d, 2 * te), lambda i, k: (0, jnp.maximum(k - 1, 0))),
                  full, full, rowv, rowv],
        out_specs=pl.BlockSpec((d, tt), lambda i, k: (0, i)),
        out_shape=jax.ShapeDtypeStruct((d, t), jnp.float32),
        scratch_shapes=[pltpu.VMEM((te, tt), jnp.float32), pltpu.VMEM((te, tt), jnp.float32),
                        pltpu.VMEM((te, tt), jnp.bfloat16), pltpu.VMEM((te, tt), jnp.bfloat16)],
        compiler_params=_params(("parallel", "arbitrary")),
        name="peer_experts",
    )(xnt, u, vt, s0, s1, thr, nrm)


def _final_kernel(h_ref, pt_ref, g_ref, o_ref):
    x = h_ref[...] + pt_ref[...].T
    o_ref[...] = x * lax.rsqrt(jnp.mean(x * x, axis=-1, keepdims=True) + NORM_EPS) * g_ref[...]


def _final(h, peer_t, g, *, tm=256):
    s, d = h.shape
    return pl.pallas_call(
        _final_kernel,
        grid=(s // tm,),
        in_specs=[pl.BlockSpec((tm, d), lambda i: (i, 0)),
                  pl.BlockSpec((d, tm), lambda i: (0, i)),
                  pl.BlockSpec((1, d), lambda i: (0, 0))],
        out_specs=pl.BlockSpec((tm, d), lambda i: (i, 0)),
        out_shape=jax.ShapeDtypeStruct((s, d), jnp.float32),
        compiler_params=_params(("parallel",)),
        name="final_norm",
    )(h, peer_t, g.reshape(1, d))


def kernel(x, norm_mix_g, w_in, w_out, rel_bias, norm_ffn_g, peer_w_q, peer_sub_keys,
           peer_u, peer_v, norm_final_g):
    b, s, d = x.shape
    assert b == 1 and d == D_MODEL and s % RET_CHUNK == 0 and norm_mix_g.shape[0] == 1
    bf16 = jnp.bfloat16
    h = x.reshape(s, d)

    half = RET_HEAD_DIM // 2
    inv = ROPE_BASE ** (-jnp.arange(half, dtype=jnp.float32) / half)
    ang = jnp.arange(s, dtype=jnp.int32).astype(jnp.float32)[:, None] * inv[None, :]
    cos, sin = jnp.cos(ang), jnp.sin(ang)
    log_g = jnp.log1p(-(2.0 ** (-5.0 - jnp.arange(RET_HEADS, dtype=jnp.float32))))

    xn = _rmsnorm(h, norm_mix_g[0], transpose=False)
    proj = _inproj(xn, w_in[0].astype(bf16))
    a_out = _moba(proj, _kmean(proj), _bias_tables(rel_bias))
    r_out = _retention(proj, log_g, cos, sin)
    h = _outproj(a_out, r_out, w_out[0].astype(bf16), h)

    xnt = _rmsnorm(h, norm_ffn_g[0], transpose=True)
    keys = peer_sub_keys[0].astype(bf16).reshape(PEER_HEADS * 2, PEER_NKEYS, PEER_QDIM // 2)
    s0, s1, thr, nrm = _peer_prep(xnt, peer_w_q[0].T.astype(bf16), keys)
    peer_t = _peer(xnt, peer_u[0].astype(bf16), peer_v[0].T.astype(bf16), s0, s1, thr, nrm)
    y = _final(h, peer_t, norm_final_g)
    return y.reshape(b, s, d)
```

```python
import functools
import math

import numpy as np
import jax
import jax.numpy as jnp
from jax import lax
from jax.experimental import pallas as pl
from jax.experimental.pallas import tpu as pltpu

D_MODEL = 4096
MOBA_HEADS = 16
MOBA_HEAD_DIM = 128
MOBA_WIDTH = MOBA_HEADS * MOBA_HEAD_DIM
MOBA_BLOCK = 256
MOBA_TOPK = 3
RET_HEADS = 8
RET_HEAD_DIM = 256
RET_WIDTH = RET_HEADS * RET_HEAD_DIM
RET_CHUNK = 512
ROPE_BASE = 10000.0
REL_BUCKETS = 32
REL_MAX_DIST = 128
IN_WIDTH = 3 * MOBA_WIDTH + 4 * RET_WIDTH
PEER_HEADS = 8
PEER_NKEYS = 128
PEER_EXPERTS = PEER_NKEYS * PEER_NKEYS
PEER_QDIM = 256
PEER_TOPK = 16
NORM_EPS = 1e-6
GN_EPS = 1e-6
NEG = -1e30

LOG2E = math.log2(math.e)
MOBA_HEADS_PER_STEP = 2
MOBA_BLOCKS_PER_ITER = 8
LANES = 128
N_SLABS = IN_WIDTH // LANES
VMEM_LIMIT = 56 * 1024 * 1024

SLAB_MQ = 0
SLAB_MK = MOBA_WIDTH // LANES
SLAB_MV = 2 * MOBA_WIDTH // LANES
SLAB_RQ = 3 * MOBA_WIDTH // LANES
SLAB_RK = SLAB_RQ + RET_WIDTH // LANES
SLAB_RV = SLAB_RK + RET_WIDTH // LANES
SLAB_RG = SLAB_RV + RET_WIDTH // LANES


def _params(sem, vmem=VMEM_LIMIT):
    return pltpu.CompilerParams(dimension_semantics=sem, vmem_limit_bytes=vmem)


def _dot(a, b):
    return jnp.dot(a, b, preferred_element_type=jnp.float32)


def _dot_nt(a, b):
    return lax.dot_general(a, b, (((1,), (1,)), ((), ())), preferred_element_type=jnp.float32)


def _rmsnorm_kernel(x_ref, g_ref, o_ref, *, transpose):
    x = x_ref[...]
    y = x * lax.rsqrt(jnp.mean(x * x, axis=-1, keepdims=True) + NORM_EPS) * g_ref[...]
    if transpose:
        y = y.T
    o_ref[...] = y.astype(o_ref.dtype)


def _rmsnorm(x, g, *, transpose, tm=256):
    s, d = x.shape
    if transpose:
        out_shape = jax.ShapeDtypeStruct((d, s), jnp.bfloat16)
        out_spec = pl.BlockSpec((d, tm), lambda i: (0, i))
    else:
        out_shape = jax.ShapeDtypeStruct((s, d), jnp.bfloat16)
        out_spec = pl.BlockSpec((tm, d), lambda i: (i, 0))
    return pl.pallas_call(
        functools.partial(_rmsnorm_kernel, transpose=transpose),
        grid=(s // tm,),
        in_specs=[pl.BlockSpec((tm, d), lambda i: (i, 0)),
                  pl.BlockSpec((1, d), lambda i: (0, 0))],
        out_specs=out_spec,
        out_shape=out_shape,
        compiler_params=_params(("parallel",)),
        name="rmsnorm_t" if transpose else "rmsnorm",
    )(x, g.reshape(1, d))


def _inproj_kernel(a_ref, b_ref, o_ref, w_ref):
    @pl.when(pl.program_id(1) == 0)
    def _():
        w_ref[...] = b_ref[...].astype(w_ref.dtype)

    acc = _dot(a_ref[...], w_ref[...])
    for c in range(o_ref.shape[0]):
        o_ref[c] = acc[:, c * LANES:(c + 1) * LANES].astype(o_ref.dtype)


def _inproj(xn, w, *, tm=512, tn=1024):
    s, d = xn.shape
    n = w.shape[1]
    return pl.pallas_call(
        _inproj_kernel,
        grid=(n // tn, s // tm),
        in_specs=[pl.BlockSpec((tm, d), lambda j, i: (i, 0)),
                  pl.BlockSpec((d, tn), lambda j, i: (0, j))],
        out_specs=pl.BlockSpec((tn // LANES, tm, LANES), lambda j, i: (j, i, 0)),
        out_shape=jax.ShapeDtypeStruct((n // LANES, s, LANES), jnp.bfloat16),
        scratch_shapes=[pltpu.VMEM((d, tn), jnp.bfloat16)],
        compiler_params=_params(("parallel", "arbitrary")),
        name="inproj",
    )(xn, w)


def _kmean_kernel(k_ref, o_ref):
    k = k_ref[0].astype(jnp.float32)
    nblk = k.shape[0] // MOBA_BLOCK
    km = jnp.mean(k.reshape(nblk, MOBA_BLOCK, LANES), axis=1)
    pad = jnp.zeros((LANES - nblk, LANES), jnp.float32)
    o_ref[0] = jnp.concatenate([km, pad], axis=0)


def _kmean(proj):
    s = proj.shape[1]
    return pl.pallas_call(
        _kmean_kernel,
        grid=(MOBA_HEADS,),
        in_specs=[pl.BlockSpec((1, s, LANES), lambda h: (SLAB_MK + h, 0, 0))],
        out_specs=pl.BlockSpec((1, LANES, LANES), lambda h: (h, 0, 0)),
        out_shape=jax.ShapeDtypeStruct((MOBA_HEADS, LANES, LANES), jnp.float32),
        compiler_params=_params(("parallel",)),
        name="moba_kmean",
    )(proj)


def _t5_bucket_starts():
    max_exact = REL_BUCKETS // 2
    d = np.arange(0, REL_MAX_DIST + 1)
    nf = np.maximum(d, 1).astype(np.float64)
    large = max_exact + (np.log(nf / max_exact) / math.log(REL_MAX_DIST / max_exact)
                         * (REL_BUCKETS - max_exact)).astype(np.int64)
    large = np.minimum(large, REL_BUCKETS - 1)
    bucket = np.where(d < max_exact, d, large)
    return [int(np.argmax(bucket >= b)) for b in range(REL_BUCKETS)]


_BUCKET_START = _t5_bucket_starts()


def _bias_kernel(rb_ref, o_ref):
    h = pl.program_id(0)
    qi = lax.broadcasted_iota(jnp.int32, (MOBA_BLOCK, MOBA_BLOCK), 0)
    ki = lax.broadcasted_iota(jnp.int32, (MOBA_BLOCK, MOBA_BLOCK), 1)
    for r in range(3):
        d = qi - ki + r * MOBA_BLOCK
        dd = jnp.maximum(d, 0)
        bias = jnp.full((MOBA_BLOCK, MOBA_BLOCK), rb_ref[0, h], jnp.float32)
        for b in range(1, REL_BUCKETS):
            bias = jnp.where(dd >= _BUCKET_START[b], rb_ref[b, h], bias)
        o_ref[0, r] = jnp.where(d >= 0, bias * LOG2E, NEG)


def _bias_tables(rel_bias):
    return pl.pallas_call(
        _bias_kernel,
        grid=(MOBA_HEADS,),
        in_specs=[pl.BlockSpec(memory_space=pltpu.SMEM)],
        out_specs=pl.BlockSpec((1, 3, MOBA_BLOCK, MOBA_BLOCK), lambda h: (h, 0, 0, 0)),
        out_shape=jax.ShapeDtypeStruct((MOBA_HEADS, 3, MOBA_BLOCK, MOBA_BLOCK), jnp.float32),
        compiler_params=_params(("arbitrary",)),
        name="moba_bias",
    )(rel_bias)


def _moba_kernel(q_ref, k_ref, v_ref, km_ref, tbl_ref, o_ref, s_ref):
    qi = pl.program_id(1)
    blk = MOBA_BLOCK
    heads = q_ref.shape[0]
    score_scale = MOBA_HEAD_DIM ** -0.5 * LOG2E
    row = lax.broadcasted_iota(jnp.int32, (LANES, blk), 0)
    rowf = row.astype(jnp.float32)
    valid = row < qi
    lane = lax.broadcasted_iota(jnp.int32, (blk, LANES), 1)

    def augmented_query(hh):
        q = q_ref[hh]
        gate = _dot_nt(km_ref[hh].astype(jnp.bfloat16), q)
        g = jnp.where(valid, gate, NEG)
        sel = row == qi
        for _ in range(MOBA_TOPK):
            m = jnp.max(g, axis=0, keepdims=True)
            idx = jnp.min(jnp.where(g == m, rowf, float(LANES)), axis=0, keepdims=True)
            pick = rowf == idx
            sel = sel | (pick & valid)
            g = jnp.where(pick, -jnp.inf, g)
        mask_bias = jnp.where(sel, 0.0, NEG).T.astype(jnp.bfloat16)
        return jnp.concatenate([q, mask_bias], axis=1)

    q_aug = [augmented_query(hh) for hh in range(heads)]

    grp = MOBA_BLOCKS_PER_ITER
    n_groups = (qi + grp) // grp

    def group_rows(t):
        js = [qi - grp * t - g for g in range(grp)]
        rows = [pl.ds(pl.multiple_of(jnp.maximum(j, 0) * blk, blk), blk) for j in js]
        return js, rows

    def fold(x):
        return [x[:, c * LANES:(c + 1) * LANES] for c in range(grp * blk // LANES)]

    def scores(t, mpart):
        js, rows = group_rows(t)
        hot = [(lane == jnp.where(j >= 0, j, LANES - 1)).astype(jnp.bfloat16) for j in js]
        rel = [jnp.minimum(grp * t + g, 2) for g in range(grp)]
        out = []
        for hh in range(heads):
            k_aug = jnp.concatenate(
                [jnp.concatenate([k_ref[hh, rows[g], :], hot[g]], axis=1) for g in range(grp)],
                axis=0)
            bias = jnp.concatenate([tbl_ref[hh, rel[g]] for g in range(grp)], axis=1)
            s = _dot_nt(q_aug[hh], k_aug) * score_scale + bias
            s_ref[hh, t] = s
            out.append(functools.reduce(jnp.maximum, fold(s), mpart[hh]))
        return tuple(out)

    mpart = lax.fori_loop(0, n_groups, scores,
                          tuple(jnp.full((blk, LANES), -jnp.inf, jnp.float32) for _ in range(heads)))
    m_row = [jnp.max(mp, axis=1, keepdims=True) for mp in mpart]

    def values(t, carry):
        _, rows = group_rows(t)
        out = []
        for hh in range(heads):
            lpart, acc = carry[hh]
            p = jnp.exp2(s_ref[hh, t] - m_row[hh])
            vv = jnp.concatenate([v_ref[hh, rows[g], :] for g in range(grp)], axis=0)
            out.append((functools.reduce(jnp.add, fold(p), lpart),
                        acc + _dot(p.astype(jnp.bfloat16), vv)))
        return tuple(out)

    zero = jnp.zeros((blk, LANES), jnp.float32)
    fin = lax.fori_loop(0, n_groups, values, tuple((zero, zero) for _ in range(heads)))
    for hh in range(heads):
        lpart, acc = fin[hh]
        denom = jnp.sum(lpart, axis=1, keepdims=True)
        o_ref[:, hh * LANES:(hh + 1) * LANES] = (acc / denom).astype(o_ref.dtype)


def _moba(proj, kmean, tables, *, heads=MOBA_HEADS_PER_STEP):
    s = proj.shape[1]
    nq = s // MOBA_BLOCK
    assert nq < LANES - 1 and MOBA_HEADS % heads == 0
    slab = lambda base: (lambda h, i: (base // heads + h, 0, 0))
    return pl.pallas_call(
        _moba_kernel,
        grid=(MOBA_HEADS // heads, nq),
        in_specs=[pl.BlockSpec((heads, MOBA_BLOCK, LANES), lambda h, i: (SLAB_MQ // heads + h, i, 0)),
                  pl.BlockSpec((heads, s, LANES), slab(SLAB_MK)),
                  pl.BlockSpec((heads, s, LANES), slab(SLAB_MV)),
                  pl.BlockSpec((heads, LANES, LANES), slab(0)),
                  pl.BlockSpec((heads, 3, MOBA_BLOCK, MOBA_BLOCK), lambda h, i: (h, 0, 0, 0))],
        out_specs=pl.BlockSpec((MOBA_BLOCK, heads * LANES), lambda h, i: (i, h)),
        out_shape=jax.ShapeDtypeStruct((s, MOBA_WIDTH), jnp.bfloat16),
        scratch_shapes=[pltpu.VMEM((heads, pl.cdiv(nq, MOBA_BLOCKS_PER_ITER), MOBA_BLOCK,
                                    MOBA_BLOCKS_PER_ITER * MOBA_BLOCK), jnp.float32)],
        compiler_params=_params(("parallel", "parallel")),
        name="moba_attn",
    )(proj, proj, proj, kmean, tables)


def _retention_kernel(lg_ref, q_ref, k_ref, v_ref, g_ref, cos_ref, sin_ref, o_ref,
                      state_ref, decay_ref):
    h = pl.program_id(0)
    c = pl.program_id(1)
    cs = RET_CHUNK
    lg = lg_ref[h]

    @pl.when(c == 0)
    def _():
        state_ref[...] = jnp.zeros_like(state_ref)
        i = lax.broadcasted_iota(jnp.int32, (cs, cs), 0)
        j = lax.broadcasted_iota(jnp.int32, (cs, cs), 1)
        diff = (i - j).astype(jnp.float32)
        decay_ref[...] = jnp.where(diff >= 0, jnp.exp(lg * jnp.maximum(diff, 0.0)), 0.0)

    cos = cos_ref[...]
    sin = sin_ref[...]

    def rope(ref):
        x1 = ref[0].astype(jnp.float32)
        x2 = ref[1].astype(jnp.float32)
        return jnp.concatenate([x1 * cos - x2 * sin, x1 * sin + x2 * cos], axis=1)

    rq = rope(q_ref)
    rk = rope(k_ref) * (RET_HEAD_DIM ** -0.5)
    v = jnp.concatenate([v_ref[0], v_ref[1]], axis=1)
    row = lax.broadcasted_iota(jnp.int32, (cs, 1), 0).astype(jnp.float32)
    q_decay = jnp.exp(lg * (row + 1.0))
    k_decay = jnp.exp(lg * (cs - 1.0 - row))

    state = state_ref[...]
    inner = _dot_nt(rq.astype(jnp.bfloat16), rk.astype(jnp.bfloat16)) * decay_ref[...]
    o = (_dot(inner.astype(jnp.bfloat16), v)
         + _dot((rq * q_decay).astype(jnp.bfloat16), state.astype(jnp.bfloat16)))
    kd_t = (rk * k_decay).T.astype(jnp.bfloat16)
    state_ref[...] = state * jnp.exp(lg * cs) + _dot(kd_t, v)

    mu = jnp.mean(o, axis=-1, keepdims=True)
    ctr = o - mu
    var = jnp.mean(ctr * ctr, axis=-1, keepdims=True)
    y = ctr * lax.rsqrt(var + GN_EPS)
    gate = jnp.concatenate([g_ref[0], g_ref[1]], axis=1).astype(jnp.float32)
    o_ref[...] = (gate * jax.nn.sigmoid(gate) * y).astype(o_ref.dtype)


def _retention(proj, log_g, cos, sin):
    s = proj.shape[1]
    cs = RET_CHUNK
    pair = lambda base: pl.BlockSpec((2, cs, LANES), lambda h, c: (base // 2 + h, c, 0))
    return pl.pallas_call(
        _retention_kernel,
        grid=(RET_HEADS, s // cs),
        in_specs=[pl.BlockSpec(memory_space=pltpu.SMEM),
                  pair(SLAB_RQ), pair(SLAB_RK), pair(SLAB_RV), pair(SLAB_RG),
                  pl.BlockSpec((cs, LANES), lambda h, c: (c, 0)),
                  pl.BlockSpec((cs, LANES), lambda h, c: (c, 0))],
        out_specs=pl.BlockSpec((cs, RET_HEAD_DIM), lambda h, c: (c, h)),
        out_shape=jax.ShapeDtypeStruct((s, RET_WIDTH), jnp.bfloat16),
        scratch_shapes=[pltpu.VMEM((RET_HEAD_DIM, RET_HEAD_DIM), jnp.float32),
                        pltpu.VMEM((cs, cs), jnp.float32)],
        compiler_params=_params(("parallel", "arbitrary")),
        name="retention",
    )(log_g, proj, proj, proj, proj, cos, sin)


def _outproj_kernel(a_ref, r_ref, wa_ref, wr_ref, x_ref, o_ref):
    o_ref[...] = x_ref[...] + _dot(a_ref[...], wa_ref[...]) + _dot(r_ref[...], wr_ref[...])


def _outproj(a_out, r_out, w_out, x, *, tm=512, tn=1024):
    s, d = x.shape
    ka = a_out.shape[1]
    kr = r_out.shape[1]
    return pl.pallas_call(
        _outproj_kernel,
        grid=(d // tn, s // tm),
        in_specs=[pl.BlockSpec((tm, ka), lambda j, i: (i, 0)),
                  pl.BlockSpec((tm, kr), lambda j, i: (i, 0)),
                  pl.BlockSpec((ka, tn), lambda j, i: (0, j)),
                  pl.BlockSpec((kr, tn), lambda j, i: (ka // kr, j)),
                  pl.BlockSpec((tm, tn), lambda j, i: (i, j))],
        out_specs=pl.BlockSpec((tm, tn), lambda j, i: (i, j)),
        out_shape=jax.ShapeDtypeStruct((s, d), jnp.float32),
        compiler_params=_params(("parallel", "parallel")),
        name="outproj",
    )(a_out, r_out, w_out, w_out, x)


def _knockout_topk(s, k, on_value):
    r = s.shape[0]
    rid = lax.broadcasted_iota(jnp.int32, s.shape, 0).astype(jnp.float32)
    for i in range(k):
        m = jnp.max(s, axis=0, keepdims=True)
        on_value(i, m)
        if i + 1 < k:
            idx = jnp.min(jnp.where(s == m, rid, float(r)), axis=0, keepdims=True)
            s = jnp.where(rid == idx, -jnp.inf, s)


def _peer_prep_kernel(xt_ref, wq_ref, keys_ref, s0_ref, s1_ref, thr_ref, nrm_ref, a_ref, b_ref):
    nk = PEER_NKEYS
    qt = _dot(wq_ref[...], xt_ref[...]).astype(jnp.bfloat16)
    s0 = _dot(keys_ref[0], qt[:nk])
    s1 = _dot(keys_ref[1], qt[nk:])
    s0_ref[0] = s0
    s1_ref[0] = s1

    def put(ref):
        def f(i, m):
            ref[i:i + 1, :] = m
        return f

    _knockout_topk(s0, PEER_TOPK, put(a_ref))
    _knockout_topk(s1, PEER_TOPK, put(b_ref))
    a = a_ref[...]
    b = b_ref[...]
    half = PEER_TOPK // 2
    cands = [a[0:1] + b]
    for i in range(1, half):
        cands.append(a[i:i + 1] + b[0:half])
    cands.append(a[half:] + b[0:1])
    cand = jnp.concatenate(cands, axis=0)

    top = a[0:1] + b[0:1]
    stats = {"z": jnp.zeros_like(top), "thr": top}

    def acc(i, m):
        stats["z"] = stats["z"] + jnp.exp(m - top)
        stats["thr"] = m

    _knockout_topk(cand, PEER_TOPK, acc)
    thr_ref[0] = stats["thr"]
    nrm_ref[0] = top + jnp.log(stats["z"])


def _peer_prep(xnt, wqt, keys, *, tt=512):
    d, t = xnt.shape
    nk = PEER_NKEYS
    sc_spec = pl.BlockSpec((1, nk, tt), lambda i, h: (h, 0, i))
    row_spec = pl.BlockSpec((1, 1, tt), lambda i, h: (h, 0, i))
    sc_shape = jax.ShapeDtypeStruct((PEER_HEADS, nk, t), jnp.float32)
    row_shape = jax.ShapeDtypeStruct((PEER_HEADS, 1, t), jnp.float32)
    return pl.pallas_call(
        _peer_prep_kernel,
        grid=(t // tt, PEER_HEADS),
        in_specs=[pl.BlockSpec((d, tt), lambda i, h: (0, i)),
                  pl.BlockSpec((PEER_QDIM, d), lambda i, h: (h, 0)),
                  pl.BlockSpec((2, nk, nk), lambda i, h: (h, 0, 0))],
        out_specs=[sc_spec, sc_spec, row_spec, row_spec],
        out_shape=[sc_shape, sc_shape, row_shape, row_shape],
        scratch_shapes=[pltpu.VMEM((PEER_TOPK, tt), jnp.float32),
                        pltpu.VMEM((PEER_TOPK, tt), jnp.float32)],
        compiler_params=_params(("parallel", "arbitrary")),
        name="peer_prep",
    )(xnt, wqt, keys)


def _peer_kernel(xt_ref, u_ref, vt_ref, s0_ref, s1_ref, thr_ref, nrm_ref, o_ref,
                 pre0_ref, pre1_ref, gw0_ref, gw1_ref, *, n_tiles):
    k = pl.program_id(1)
    nk = PEER_NKEYS
    te = pre0_ref.shape[0]
    rows = te // nk

    @pl.when(k == 0)
    def _():
        o_ref[...] = jnp.zeros_like(o_ref)
        pre1_ref[...] = jnp.zeros_like(pre1_ref)
        gw0_ref[...] = jnp.zeros_like(gw0_ref)

    def half_step(half, gate_tile, gw_in, pre_out, pre_in, gw_out):
        cols = slice(half * te, (half + 1) * te)
        o_ref[...] += _dot(vt_ref[:, cols], gw_in[...])
        pre_out[...] = _dot(u_ref[cols, :], xt_ref[...])
        for r in range(rows):
            rr = slice(r * nk, (r + 1) * nk)
            pre = pre_in[rr, :]
            act = 0.5 * pre * (1.0 + lax.erf(pre * math.sqrt(0.5)))
            w = jnp.zeros_like(pre)
            for h in range(PEER_HEADS):
                z = s0_ref[h, pl.ds(gate_tile * rows + r, 1), :] + s1_ref[h]
                w = w + jnp.where(z >= thr_ref[h], jnp.exp(z - nrm_ref[h]), 0.0)
            gw_out[rr, :] = (act * w).astype(gw_out.dtype)

    last = n_tiles - 1
    half_step(0, jnp.clip(2 * k - 1, 0, last), gw0_ref, pre0_ref, pre1_ref, gw1_ref)
    half_step(1, jnp.minimum(2 * k, last), gw1_ref, pre1_ref, pre0_ref, gw0_ref)


def _peer(xnt, u, vt, s0, s1, thr, nrm, *, tt=512, te=256):
    d, t = xnt.shape
    e = u.shape[0]
    nk = PEER_NKEYS
    n_tiles = e // te
    n_steps = n_tiles // 2 + 1
    full = pl.BlockSpec((PEER_HEADS, nk, tt), lambda i, k: (0, 0, i))
    rowv = pl.BlockSpec((PEER_HEADS, 1, tt), lambda i, k: (0, 0, i))
    return pl.pallas_call(
        functools.partial(_peer_kernel, n_tiles=n_tiles),
        grid=(t // tt, n_steps),
        in_specs=[pl.BlockSpec((d, tt), lambda i, k: (0, i)),
                  pl.BlockSpec((2 * te, d), lambda i, k: (jnp.minimum(k, n_steps - 2), 0)),
                  pl.BlockSpec((d, 2 * te), lambda i, k: (0, jnp.maximum(k - 1, 0))),
                  full, full, rowv, rowv],
        out_specs=pl.BlockSpec((d, tt), lambda i, k: (0, i)),
        out_shape=jax.ShapeDtypeStruct((d, t), jnp.float32),
        scratch_shapes=[pltpu.VMEM((te, tt), jnp.float32), pltpu.VMEM((te, tt), jnp.float32),
                        pltpu.VMEM((te, tt), jnp.bfloat16), pltpu.VMEM((te, tt), jnp.bfloat16)],
        compiler_params=_params(("parallel", "arbitrary")),
        name="peer_experts",
    )(xnt, u, vt, s0, s1, thr, nrm)


def _final_kernel(h_ref, pt_ref, g_ref, o_ref):
    x = h_ref[...] + pt_ref[...].T
    o_ref[...] = x * lax.rsqrt(jnp.mean(x * x, axis=-1, keepdims=True) + NORM_EPS) * g_ref[...]


def _final(h, peer_t, g, *, tm=256):
    s, d = h.shape
    return pl.pallas_call(
        _final_kernel,
        grid=(s // tm,),
        in_specs=[pl.BlockSpec((tm, d), lambda i: (i, 0)),
                  pl.BlockSpec((d, tm), lambda i: (0, i)),
                  pl.BlockSpec((1, d), lambda i: (0, 0))],
        out_specs=pl.BlockSpec((tm, d), lambda i: (i, 0)),
        out_shape=jax.ShapeDtypeStruct((s, d), jnp.float32),
        compiler_params=_params(("parallel",)),
        name="final_norm",
    )(h, peer_t, g.reshape(1, d))


def kernel(x, norm_mix_g, w_in, w_out, rel_bias, norm_ffn_g, peer_w_q, peer_sub_keys,
           peer_u, peer_v, norm_final_g):
    b, s, d = x.shape
    assert b == 1 and d == D_MODEL and s % RET_CHUNK == 0 and norm_mix_g.shape[0] == 1
    bf16 = jnp.bfloat16
    h = x.reshape(s, d)

    half = RET_HEAD_DIM // 2
    inv = ROPE_BASE ** (-jnp.arange(half, dtype=jnp.float32) / half)
    ang = jnp.arange(s, dtype=jnp.int32).astype(jnp.float32)[:, None] * inv[None, :]
    cos, sin = jnp.cos(ang), jnp.sin(ang)
    log_g = jnp.log1p(-(2.0 ** (-5.0 - jnp.arange(RET_HEADS, dtype=jnp.float32))))

    xn = _rmsnorm(h, norm_mix_g[0], transpose=False)
    proj = _inproj(xn, w_in[0])
    a_out = _moba(proj, _kmean(proj), _bias_tables(rel_bias))
    r_out = _retention(proj, log_g, cos, sin)
    h = _outproj(a_out, r_out, w_out[0].astype(bf16), h)

    xnt = _rmsnorm(h, norm_ffn_g[0], transpose=True)
    keys = peer_sub_keys[0].astype(bf16).reshape(PEER_HEADS * 2, PEER_NKEYS, PEER_QDIM // 2)
    s0, s1, thr, nrm = _peer_prep(xnt, peer_w_q[0].T.astype(bf16), keys)
    peer_t = _peer(xnt, peer_u[0].astype(bf16), peer_v[0].T.astype(bf16), s0, s1, thr, nrm)
    y = _final(h, peer_t, norm_final_g)
    return y.reshape(b, s, d)
```

```python
import functools
import math

import numpy as np
import jax
import jax.numpy as jnp
from jax import lax
from jax.experimental import pallas as pl
from jax.experimental.pallas import tpu as pltpu

D_MODEL = 4096
MOBA_HEADS = 16
MOBA_HEAD_DIM = 128
MOBA_WIDTH = MOBA_HEADS * MOBA_HEAD_DIM
MOBA_BLOCK = 256
MOBA_TOPK = 3
RET_HEADS = 8
RET_HEAD_DIM = 256
RET_WIDTH = RET_HEADS * RET_HEAD_DIM
RET_CHUNK = 512
ROPE_BASE = 10000.0
REL_BUCKETS = 32
REL_MAX_DIST = 128
IN_WIDTH = 3 * MOBA_WIDTH + 4 * RET_WIDTH
PEER_HEADS = 8
PEER_NKEYS = 128
PEER_EXPERTS = PEER_NKEYS * PEER_NKEYS
PEER_QDIM = 256
PEER_TOPK = 16
NORM_EPS = 1e-6
GN_EPS = 1e-6
NEG = -1e30

LOG2E = math.log2(math.e)
MOBA_HEADS_PER_STEP = 2
MOBA_BLOCKS_PER_ITER = 8
LANES = 128
N_SLABS = IN_WIDTH // LANES
VMEM_LIMIT = 56 * 1024 * 1024

SLAB_MQ = 0
SLAB_MK = MOBA_WIDTH // LANES
SLAB_MV = 2 * MOBA_WIDTH // LANES
SLAB_RQ = 3 * MOBA_WIDTH // LANES
SLAB_RK = SLAB_RQ + RET_WIDTH // LANES
SLAB_RV = SLAB_RK + RET_WIDTH // LANES
SLAB_RG = SLAB_RV + RET_WIDTH // LANES


def _params(sem, vmem=VMEM_LIMIT):
    return pltpu.CompilerParams(dimension_semantics=sem, vmem_limit_bytes=vmem)


def _dot(a, b):
    return jnp.dot(a, b, preferred_element_type=jnp.float32)


def _dot_nt(a, b):
    return lax.dot_general(a, b, (((1,), (1,)), ((), ())), preferred_element_type=jnp.float32)


def _rmsnorm_kernel(x_ref, g_ref, o_ref, *, transpose):
    x = x_ref[...]
    y = x * lax.rsqrt(jnp.mean(x * x, axis=-1, keepdims=True) + NORM_EPS) * g_ref[...]
    if transpose:
        y = y.T
    o_ref[...] = y.astype(o_ref.dtype)


def _rmsnorm(x, g, *, transpose, tm=256):
    s, d = x.shape
    if transpose:
        out_shape = jax.ShapeDtypeStruct((d, s), jnp.bfloat16)
        out_spec = pl.BlockSpec((d, tm), lambda i: (0, i))
    else:
        out_shape = jax.ShapeDtypeStruct((s, d), jnp.bfloat16)
        out_spec = pl.BlockSpec((tm, d), lambda i: (i, 0))
    return pl.pallas_call(
        functools.partial(_rmsnorm_kernel, transpose=transpose),
        grid=(s // tm,),
        in_specs=[pl.BlockSpec((tm, d), lambda i: (i, 0)),
                  pl.BlockSpec((1, d), lambda i: (0, 0))],
        out_specs=out_spec,
        out_shape=out_shape,
        compiler_params=_params(("parallel",)),
        name="rmsnorm_t" if transpose else "rmsnorm",
    )(x, g.reshape(1, d))


def _inproj_kernel(a_ref, b_ref, o_ref, w_ref):
    @pl.when(pl.program_id(1) == 0)
    def _():
        w_ref[...] = b_ref[...].astype(w_ref.dtype)

    acc = _dot(a_ref[...], w_ref[...])
    for c in range(o_ref.shape[0]):
        o_ref[c] = acc[:, c * LANES:(c + 1) * LANES].astype(o_ref.dtype)


def _inproj(xn, w, *, tm=512, tn=1024):
    s, d = xn.shape
    n = w.shape[1]
    return pl.pallas_call(
        _inproj_kernel,
        grid=(n // tn, s // tm),
        in_specs=[pl.BlockSpec((tm, d), lambda j, i: (i, 0)),
                  pl.BlockSpec((d, tn), lambda j, i: (0, j))],
        out_specs=pl.BlockSpec((tn // LANES, tm, LANES), lambda j, i: (j, i, 0)),
        out_shape=jax.ShapeDtypeStruct((n // LANES, s, LANES), jnp.bfloat16),
        scratch_shapes=[pltpu.VMEM((d, tn), jnp.bfloat16)],
        compiler_params=_params(("parallel", "arbitrary")),
        name="inproj",
    )(xn, w)


def _kmean_kernel(k_ref, o_ref):
    k = k_ref[0].astype(jnp.float32)
    nblk = k.shape[0] // MOBA_BLOCK
    km = jnp.mean(k.reshape(nblk, MOBA_BLOCK, LANES), axis=1)
    pad = jnp.zeros((LANES - nblk, LANES), jnp.float32)
    o_ref[0] = jnp.concatenate([km, pad], axis=0)


def _kmean(proj):
    s = proj.shape[1]
    return pl.pallas_call(
        _kmean_kernel,
        grid=(MOBA_HEADS,),
        in_specs=[pl.BlockSpec((1, s, LANES), lambda h: (SLAB_MK + h, 0, 0))],
        out_specs=pl.BlockSpec((1, LANES, LANES), lambda h: (h, 0, 0)),
        out_shape=jax.ShapeDtypeStruct((MOBA_HEADS, LANES, LANES), jnp.float32),
        compiler_params=_params(("parallel",)),
        name="moba_kmean",
    )(proj)


def _t5_bucket_starts():
    max_exact = REL_BUCKETS // 2
    d = np.arange(0, REL_MAX_DIST + 1)
    nf = np.maximum(d, 1).astype(np.float64)
    large = max_exact + (np.log(nf / max_exact) / math.log(REL_MAX_DIST / max_exact)
                         * (REL_BUCKETS - max_exact)).astype(np.int64)
    large = np.minimum(large, REL_BUCKETS - 1)
    bucket = np.where(d < max_exact, d, large)
    return [int(np.argmax(bucket >= b)) for b in range(REL_BUCKETS)]


_BUCKET_START = _t5_bucket_starts()


def _bias_kernel(rb_ref, o_ref):
    h = pl.program_id(0)
    qi = lax.broadcasted_iota(jnp.int32, (MOBA_BLOCK, MOBA_BLOCK), 0)
    ki = lax.broadcasted_iota(jnp.int32, (MOBA_BLOCK, MOBA_BLOCK), 1)
    for r in range(3):
        d = qi - ki + r * MOBA_BLOCK
        dd = jnp.maximum(d, 0)
        bias = jnp.full((MOBA_BLOCK, MOBA_BLOCK), rb_ref[0, h], jnp.float32)
        for b in range(1, REL_BUCKETS):
            bias = jnp.where(dd >= _BUCKET_START[b], rb_ref[b, h], bias)
        o_ref[0, r] = jnp.where(d >= 0, bias * LOG2E, NEG)


def _bias_tables(rel_bias):
    return pl.pallas_call(
        _bias_kernel,
        grid=(MOBA_HEADS,),
        in_specs=[pl.BlockSpec(memory_space=pltpu.SMEM)],
        out_specs=pl.BlockSpec((1, 3, MOBA_BLOCK, MOBA_BLOCK), lambda h: (h, 0, 0, 0)),
        out_shape=jax.ShapeDtypeStruct((MOBA_HEADS, 3, MOBA_BLOCK, MOBA_BLOCK), jnp.float32),
        compiler_params=_params(("arbitrary",)),
        name="moba_bias",
    )(rel_bias)


def _moba_kernel(q_ref, k_ref, v_ref, km_ref, tbl_ref, o_ref, s_ref):
    qi = pl.program_id(1)
    blk = MOBA_BLOCK
    heads = q_ref.shape[0]
    score_scale = MOBA_HEAD_DIM ** -0.5 * LOG2E
    row = lax.broadcasted_iota(jnp.int32, (LANES, blk), 0)
    rowf = row.astype(jnp.float32)
    valid = row < qi
    lane = lax.broadcasted_iota(jnp.int32, (blk, LANES), 1)

    def augmented_query(hh):
        q = q_ref[hh]
        gate = _dot_nt(km_ref[hh].astype(jnp.bfloat16), q)
        g = jnp.where(valid, gate, NEG)
        sel = row == qi
        for _ in range(MOBA_TOPK):
            m = jnp.max(g, axis=0, keepdims=True)
            idx = jnp.min(jnp.where(g == m, rowf, float(LANES)), axis=0, keepdims=True)
            pick = rowf == idx
            sel = sel | (pick & valid)
            g = jnp.where(pick, -jnp.inf, g)
        mask_bias = jnp.where(sel, 0.0, NEG).T.astype(jnp.bfloat16)
        return jnp.concatenate([q, mask_bias], axis=1)

    q_aug = [augmented_query(hh) for hh in range(heads)]

    grp = MOBA_BLOCKS_PER_ITER
    n_groups = (qi + grp) // grp

    def group_rows(t):
        js = [qi - grp * t - g for g in range(grp)]
        rows = [pl.ds(pl.multiple_of(jnp.maximum(j, 0) * blk, blk), blk) for j in js]
        return js, rows

    def fold(x):
        return [x[:, c * LANES:(c + 1) * LANES] for c in range(grp * blk // LANES)]

    def scores(t, mpart):
        js, rows = group_rows(t)
        hot = [(lane == jnp.where(j >= 0, j, LANES - 1)).astype(jnp.bfloat16) for j in js]
        rel = [jnp.minimum(grp * t + g, 2) for g in range(grp)]
        out = []
        for hh in range(heads):
            k_aug = jnp.concatenate(
                [jnp.concatenate([k_ref[hh, rows[g], :], hot[g]], axis=1) for g in range(grp)],
                axis=0)
            bias = jnp.concatenate([tbl_ref[hh, rel[g]] for g in range(grp)], axis=1)
            s = _dot_nt(q_aug[hh], k_aug) * score_scale + bias
            s_ref[hh, t] = s
            out.append(functools.reduce(jnp.maximum, fold(s), mpart[hh]))
        return tuple(out)

    mpart = lax.fori_loop(0, n_groups, scores,
                          tuple(jnp.full((blk, LANES), -jnp.inf, jnp.float32) for _ in range(heads)))
    m_row = [jnp.max(mp, axis=1, keepdims=True) for mp in mpart]

    def values(t, carry):
        _, rows = group_rows(t)
        out = []
        for hh in range(heads):
            lpart, acc = carry[hh]
            p = jnp.exp2(s_ref[hh, t] - m_row[hh])
            vv = jnp.concatenate([v_ref[hh, rows[g], :] for g in range(grp)], axis=0)
            out.append((functools.reduce(jnp.add, fold(p), lpart),
                        acc + _dot(p.astype(jnp.bfloat16), vv)))
        return tuple(out)

    zero = jnp.zeros((blk, LANES), jnp.float32)
    fin = lax.fori_loop(0, n_groups, values, tuple((zero, zero) for _ in range(heads)))
    for hh in range(heads):
        lpart, acc = fin[hh]
        denom = jnp.sum(lpart, axis=1, keepdims=True)
        o_ref[:, hh * LANES:(hh + 1) * LANES] = (acc / denom).astype(o_ref.dtype)


def _moba(proj, kmean, tables, *, heads=MOBA_HEADS_PER_STEP):
    s = proj.shape[1]
    nq = s // MOBA_BLOCK
    assert nq < LANES - 1 and MOBA_HEADS % heads == 0
    slab = lambda base: (lambda h, i: (base // heads + h, 0, 0))
    return pl.pallas_call(
        _moba_kernel,
        grid=(MOBA_HEADS // heads, nq),
        in_specs=[pl.BlockSpec((heads, MOBA_BLOCK, LANES), lambda h, i: (SLAB_MQ // heads + h, i, 0)),
                  pl.BlockSpec((heads, s, LANES), slab(SLAB_MK)),
                  pl.BlockSpec((heads, s, LANES), slab(SLAB_MV)),
                  pl.BlockSpec((heads, LANES, LANES), slab(0)),
                  pl.BlockSpec((heads, 3, MOBA_BLOCK, MOBA_BLOCK), lambda h, i: (h, 0, 0, 0))],
        out_specs=pl.BlockSpec((MOBA_BLOCK, heads * LANES), lambda h, i: (i, h)),
        out_shape=jax.ShapeDtypeStruct((s, MOBA_WIDTH), jnp.bfloat16),
        scratch_shapes=[pltpu.VMEM((heads, pl.cdiv(nq, MOBA_BLOCKS_PER_ITER), MOBA_BLOCK,
                                    MOBA_BLOCKS_PER_ITER * MOBA_BLOCK), jnp.float32)],
        compiler_params=_params(("parallel", "parallel")),
        name="moba_attn",
    )(proj, proj, proj, kmean, tables)


def _retention_kernel(lg_ref, q_ref, k_ref, v_ref, g_ref, cos_ref, sin_ref, o_ref,
                      state_ref, decay_ref):
    h = pl.program_id(0)
    c = pl.program_id(1)
    cs = RET_CHUNK
    lg = lg_ref[h]

    @pl.when(c == 0)
    def _():
        state_ref[...] = jnp.zeros_like(state_ref)
        i = lax.broadcasted_iota(jnp.int32, (cs, cs), 0)
        j = lax.broadcasted_iota(jnp.int32, (cs, cs), 1)
        diff = (i - j).astype(jnp.float32)
        decay_ref[...] = jnp.where(diff >= 0, jnp.exp(lg * jnp.maximum(diff, 0.0)), 0.0)

    cos = cos_ref[...]
    sin = sin_ref[...]

    def rope(ref):
        x1 = ref[0].astype(jnp.float32)
        x2 = ref[1].astype(jnp.float32)
        return jnp.concatenate([x1 * cos - x2 * sin, x1 * sin + x2 * cos], axis=1)

    rq = rope(q_ref)
    rk = rope(k_ref) * (RET_HEAD_DIM ** -0.5)
    v = jnp.concatenate([v_ref[0], v_ref[1]], axis=1)
    row = lax.broadcasted_iota(jnp.int32, (cs, 1), 0).astype(jnp.float32)
    q_decay = jnp.exp(lg * (row + 1.0))
    k_decay = jnp.exp(lg * (cs - 1.0 - row))

    state = state_ref[...]
    inner = _dot_nt(rq.astype(jnp.bfloat16), rk.astype(jnp.bfloat16)) * decay_ref[...]
    o = (_dot(inner.astype(jnp.bfloat16), v)
         + _dot((rq * q_decay).astype(jnp.bfloat16), state.astype(jnp.bfloat16)))
    kd_t = (rk * k_decay).T.astype(jnp.bfloat16)
    state_ref[...] = state * jnp.exp(lg * cs) + _dot(kd_t, v)

    mu = jnp.mean(o, axis=-1, keepdims=True)
    ctr = o - mu
    var = jnp.mean(ctr * ctr, axis=-1, keepdims=True)
    y = ctr * lax.rsqrt(var + GN_EPS)
    gate = jnp.concatenate([g_ref[0], g_ref[1]], axis=1).astype(jnp.float32)
    o_ref[...] = (gate * jax.nn.sigmoid(gate) * y).astype(o_ref.dtype)


def _retention(proj, log_g, cos, sin):
    s = proj.shape[1]
    cs = RET_CHUNK
    pair = lambda base: pl.BlockSpec((2, cs, LANES), lambda h, c: (base // 2 + h, c, 0))
    return pl.pallas_call(
        _retention_kernel,
        grid=(RET_HEADS, s // cs),
        in_specs=[pl.BlockSpec(memory_space=pltpu.SMEM),
                  pair(SLAB_RQ), pair(SLAB_RK), pair(SLAB_RV), pair(SLAB_RG),
                  pl.BlockSpec((cs, LANES), lambda h, c: (c, 0)),
                  pl.BlockSpec((cs, LANES), lambda h, c: (c, 0))],
        out_specs=pl.BlockSpec((cs, RET_HEAD_DIM), lambda h, c: (c, h)),
        out_shape=jax.ShapeDtypeStruct((s, RET_WIDTH), jnp.bfloat16),
        scratch_shapes=[pltpu.VMEM((RET_HEAD_DIM, RET_HEAD_DIM), jnp.float32),
                        pltpu.VMEM((cs, cs), jnp.float32)],
        compiler_params=_params(("parallel", "arbitrary")),
        name="retention",
    )(log_g, proj, proj, proj, proj, cos, sin)


def _outproj_kernel(a_ref, r_ref, wa_ref, wr_ref, x_ref, o_ref):
    o_ref[...] = x_ref[...] + _dot(a_ref[...], wa_ref[...]) + _dot(r_ref[...], wr_ref[...])


def _outproj(a_out, r_out, w_out, x, *, tm=512, tn=1024):
    s, d = x.shape
    ka = a_out.shape[1]
    kr = r_out.shape[1]
    return pl.pallas_call(
        _outproj_kernel,
        grid=(d // tn, s // tm),
        in_specs=[pl.BlockSpec((tm, ka), lambda j, i: (i, 0)),
                  pl.BlockSpec((tm, kr), lambda j, i: (i, 0)),
                  pl.BlockSpec((ka, tn), lambda j, i: (0, j)),
                  pl.BlockSpec((kr, tn), lambda j, i: (ka // kr, j)),
                  pl.BlockSpec((tm, tn), lambda j, i: (i, j))],
        out_specs=pl.BlockSpec((tm, tn), lambda j, i: (i, j)),
        out_shape=jax.ShapeDtypeStruct((s, d), jnp.float32),
        compiler_params=_params(("parallel", "parallel")),
        name="outproj",
    )(a_out, r_out, w_out, w_out, x)


def _sorting_network(n):
    def merge(lo, hi, r):
        step = r * 2
        if step < hi - lo:
            yield from merge(lo, hi, step)
            yield from merge(lo + r, hi, step)
            yield from ((i, i + r) for i in range(lo + r, hi - r, step))
        else:
            yield (lo, lo + r)

    def sort(lo, hi):
        if hi - lo >= 1:
            mid = lo + (hi - lo) // 2
            yield from sort(lo, mid)
            yield from sort(mid + 1, hi)
            yield from merge(lo, hi, 1)

    return list(sort(0, n - 1))


def _topk_desc(s, k):
    sub = 8
    groups = s.shape[0] // sub
    lists = [s[g * sub:(g + 1) * sub, :] for g in range(groups)]
    groups = max(1 << (groups - 1).bit_length(), k)
    lists += [jnp.full_like(lists[0], -jnp.inf)] * (groups - len(lists))
    for i, j in _sorting_network(groups):
        lists[i], lists[j] = jnp.maximum(lists[i], lists[j]), jnp.minimum(lists[i], lists[j])
    pos = lax.broadcasted_iota(jnp.int32, lists[0].shape, 0).astype(jnp.float32)
    out = []
    for r in range(k):
        head = lists[0]
        m = jnp.max(head, axis=0, keepdims=True)
        out.append(m)
        left = k - r - 1
        if left:
            first = jnp.min(jnp.where(head == m, pos, float(sub)), axis=0, keepdims=True)
            pick = pos == first
            for g in range(left):
                lists[g] = jnp.where(pick, lists[g + 1], lists[g])
    return out


def _peer_prep_kernel(xt_ref, wq_ref, keys_ref, s0_ref, s1_ref, thr_ref, nrm_ref, a_ref, b_ref):
    nk = PEER_NKEYS
    t = xt_ref.shape[1]
    qt = _dot(wq_ref[...], xt_ref[...]).astype(jnp.bfloat16)
    s0 = _dot(keys_ref[0], qt[:nk])
    s1 = _dot(keys_ref[1], qt[nk:])
    s0_ref[0] = s0
    s1_ref[0] = s1

    for i, m in enumerate(_topk_desc(jnp.concatenate([s0, s1], axis=1), PEER_TOPK)):
        a_ref[i:i + 1, :] = m[:, :t]
        b_ref[i:i + 1, :] = m[:, t:]
    a = a_ref[...]
    b = b_ref[...]
    half = PEER_TOPK // 2
    cands = [a[0:1] + b]
    for i in range(1, half):
        cands.append(a[i:i + 1] + b[0:half])
    cands.append(a[half:] + b[0:1])
    cand = jnp.concatenate(cands, axis=0)

    tops = _topk_desc(cand, PEER_TOPK)
    z = functools.reduce(jnp.add, [jnp.exp(m - tops[0]) for m in tops])
    thr_ref[0] = tops[-1]
    nrm_ref[0] = tops[0] + jnp.log(z)


def _peer_prep(xnt, wqt, keys, *, tt=512):
    d, t = xnt.shape
    nk = PEER_NKEYS
    sc_spec = pl.BlockSpec((1, nk, tt), lambda i, h: (h, 0, i))
    row_spec = pl.BlockSpec((1, 1, tt), lambda i, h: (h, 0, i))
    sc_shape = jax.ShapeDtypeStruct((PEER_HEADS, nk, t), jnp.float32)
    row_shape = jax.ShapeDtypeStruct((PEER_HEADS, 1, t), jnp.float32)
    return pl.pallas_call(
        _peer_prep_kernel,
        grid=(t // tt, PEER_HEADS),
        in_specs=[pl.BlockSpec((d, tt), lambda i, h: (0, i)),
                  pl.BlockSpec((PEER_QDIM, d), lambda i, h: (h, 0)),
                  pl.BlockSpec((2, nk, nk), lambda i, h: (h, 0, 0))],
        out_specs=[sc_spec, sc_spec, row_spec, row_spec],
        out_shape=[sc_shape, sc_shape, row_shape, row_shape],
        scratch_shapes=[pltpu.VMEM((PEER_TOPK, tt), jnp.float32),
                        pltpu.VMEM((PEER_TOPK, tt), jnp.float32)],
        compiler_params=_params(("parallel", "arbitrary")),
        name="peer_prep",
    )(xnt, wqt, keys)


def _peer_kernel(xt_ref, u_ref, vt_ref, s0_ref, s1_ref, thr_ref, nrm_ref, o_ref,
                 pre0_ref, pre1_ref, gw0_ref, gw1_ref, *, n_tiles):
    k = pl.program_id(1)
    nk = PEER_NKEYS
    te = pre0_ref.shape[0]
    rows = te // nk

    @pl.when(k == 0)
    def _():
        o_ref[...] = jnp.zeros_like(o_ref)
        pre1_ref[...] = jnp.zeros_like(pre1_ref)
        gw0_ref[...] = jnp.zeros_like(gw0_ref)

    def half_step(half, gate_tile, gw_in, pre_out, pre_in, gw_out):
        cols = slice(half * te, (half + 1) * te)
        o_ref[...] += _dot(vt_ref[:, cols], gw_in[...])
        pre_out[...] = _dot(u_ref[cols, :], xt_ref[...])
        for r in range(rows):
            rr = slice(r * nk, (r + 1) * nk)
            pre = pre_in[rr, :]
            act = 0.5 * pre * (1.0 + lax.erf(pre * math.sqrt(0.5)))
            w = jnp.zeros_like(pre)
            for h in range(PEER_HEADS):
                z = s0_ref[h, pl.ds(gate_tile * rows + r, 1), :] + s1_ref[h]
                w = w + jnp.where(z >= thr_ref[h], jnp.exp(z - nrm_ref[h]), 0.0)
            gw_out[rr, :] = (act * w).astype(gw_out.dtype)

    last = n_tiles - 1
    half_step(0, jnp.clip(2 * k - 1, 0, last), gw0_ref, pre0_ref, pre1_ref, gw1_ref)
    half_step(1, jnp.minimum(2 * k, last), gw1_ref, pre1_ref, pre0_ref, gw0_ref)


def _peer(xnt, u, vt, s0, s1, thr, nrm, *, tt=512, te=256):
    d, t = xnt.shape
    e = u.shape[0]
    nk = PEER_NKEYS
    n_tiles = e // te
    n_steps = n_tiles // 2 + 1
    full = pl.BlockSpec((PEER_HEADS, nk, tt), lambda i, k: (0, 0, i))
    rowv = pl.BlockSpec((PEER_HEADS, 1, tt), lambda i, k: (0, 0, i))
    return pl.pallas_call(
        functools.partial(_peer_kernel, n_tiles=n_tiles),
        grid=(t // tt, n_steps),
        in_specs=[pl.BlockSpec((d, tt), lambda i, k: (0, i)),
                  pl.BlockSpec((2 * te, d), lambda i, k: (jnp.minimum(k, n_steps - 2), 0)),
                  pl.BlockSpec((d, 2 * te), lambda i, k: (0, jnp.maximum(k - 1, 0))),
                  full, full, rowv, rowv],
        out_specs=pl.BlockSpec((d, tt), lambda i, k: (0, i)),
        out_shape=jax.ShapeDtypeStruct((d, t), jnp.float32),
        scratch_shapes=[pltpu.VMEM((te, tt), jnp.float32), pltpu.VMEM((te, tt), jnp.float32),
                        pltpu.VMEM((te, tt), jnp.bfloat16), pltpu.VMEM((te, tt), jnp.bfloat16)],
        compiler_params=_params(("parallel", "arbitrary")),
        name="peer_experts",
    )(xnt, u, vt, s0, s1, thr, nrm)


def _final_kernel(h_ref, pt_ref, g_ref, o_ref):
    x = h_ref[...] + pt_ref[...].T
    o_ref[...] = x * lax.rsqrt(jnp.mean(x * x, axis=-1, keepdims=True) + NORM_EPS) * g_ref[...]


def _final(h, peer_t, g, *, tm=256):
    s, d = h.shape
    return pl.pallas_call(
        _final_kernel,
        grid=(s // tm,),
        in_specs=[pl.BlockSpec((tm, d), lambda i: (i, 0)),
                  pl.BlockSpec((d, tm), lambda i: (0, i)),
                  pl.BlockSpec((1, d), lambda i: (0, 0))],
        out_specs=pl.BlockSpec((tm, d), lambda i: (i, 0)),
        out_shape=jax.ShapeDtypeStruct((s, d), jnp.float32),
        compiler_params=_params(("parallel",)),
        name="final_norm",
    )(h, peer_t, g.reshape(1, d))


def kernel(x, norm_mix_g, w_in, w_out, rel_bias, norm_ffn_g, peer_w_q, peer_sub_keys,
           peer_u, peer_v, norm_final_g):
    b, s, d = x.shape
    assert b == 1 and d == D_MODEL and s % RET_CHUNK == 0 and norm_mix_g.shape[0] == 1
    bf16 = jnp.bfloat16
    h = x.reshape(s, d)

    half = RET_HEAD_DIM // 2
    inv = ROPE_BASE ** (-jnp.arange(half, dtype=jnp.float32) / half)
    ang = jnp.arange(s, dtype=jnp.int32).astype(jnp.float32)[:, None] * inv[None, :]
    cos, sin = jnp.cos(ang), jnp.sin(ang)
    log_g = jnp.log1p(-(2.0 ** (-5.0 - jnp.arange(RET_HEADS, dtype=jnp.float32))))

    xn = _rmsnorm(h, norm_mix_g[0], transpose=False)
    proj = _inproj(xn, w_in[0])
    a_out = _moba(proj, _kmean(proj), _bias_tables(rel_bias))
    r_out = _retention(proj, log_g, cos, sin)
    h = _outproj(a_out, r_out, w_out[0].astype(bf16), h)

    xnt = _rmsnorm(h, norm_ffn_g[0], transpose=True)
    keys = peer_sub_keys[0].astype(bf16).reshape(PEER_HEADS * 2, PEER_NKEYS, PEER_QDIM // 2)
    s0, s1, thr, nrm = _peer_prep(xnt, peer_w_q[0].T.astype(bf16), keys)
    peer_t = _peer(xnt, peer_u[0].astype(bf16), peer_v[0].T.astype(bf16), s0, s1, thr, nrm)
    y = _final(h, peer_t, norm_final_g)
    return y.reshape(b, s, d)
```

```python
import functools
import math

import numpy as np
import jax
import jax.numpy as jnp
from jax import lax
from jax.experimental import pallas as pl
from jax.experimental.pallas import tpu as pltpu

D_MODEL = 4096
MOBA_HEADS = 16
MOBA_HEAD_DIM = 128
MOBA_WIDTH = MOBA_HEADS * MOBA_HEAD_DIM
MOBA_BLOCK = 256
MOBA_TOPK = 3
RET_HEADS = 8
RET_HEAD_DIM = 256
RET_WIDTH = RET_HEADS * RET_HEAD_DIM
RET_CHUNK = 512
ROPE_BASE = 10000.0
REL_BUCKETS = 32
REL_MAX_DIST = 128
IN_WIDTH = 3 * MOBA_WIDTH + 4 * RET_WIDTH
PEER_HEADS = 8
PEER_NKEYS = 128
PEER_EXPERTS = PEER_NKEYS * PEER_NKEYS
PEER_QDIM = 256
PEER_TOPK = 16
NORM_EPS = 1e-6
GN_EPS = 1e-6
NEG = -1e30

LOG2E = math.log2(math.e)
MOBA_HEADS_PER_STEP = 2
MOBA_BLOCKS_PER_ITER = 8
LANES = 128
N_SLABS = IN_WIDTH // LANES
VMEM_LIMIT = 56 * 1024 * 1024

SLAB_MQ = 0
SLAB_MK = MOBA_WIDTH // LANES
SLAB_MV = 2 * MOBA_WIDTH // LANES
SLAB_RQ = 3 * MOBA_WIDTH // LANES
SLAB_RK = SLAB_RQ + RET_WIDTH // LANES
SLAB_RV = SLAB_RK + RET_WIDTH // LANES
SLAB_RG = SLAB_RV + RET_WIDTH // LANES


def _params(sem, vmem=VMEM_LIMIT):
    return pltpu.CompilerParams(dimension_semantics=sem, vmem_limit_bytes=vmem)


def _dot(a, b):
    return jnp.dot(a, b, preferred_element_type=jnp.float32)


def _dot_nt(a, b):
    return lax.dot_general(a, b, (((1,), (1,)), ((), ())), preferred_element_type=jnp.float32)


def _rmsnorm_kernel(x_ref, g_ref, o_ref, *, transpose):
    x = x_ref[...]
    y = x * lax.rsqrt(jnp.mean(x * x, axis=-1, keepdims=True) + NORM_EPS) * g_ref[...]
    if transpose:
        y = y.T
    o_ref[...] = y.astype(o_ref.dtype)


def _rmsnorm(x, g, *, transpose, tm=256):
    s, d = x.shape
    if transpose:
        out_shape = jax.ShapeDtypeStruct((d, s), jnp.bfloat16)
        out_spec = pl.BlockSpec((d, tm), lambda i: (0, i))
    else:
        out_shape = jax.ShapeDtypeStruct((s, d), jnp.bfloat16)
        out_spec = pl.BlockSpec((tm, d), lambda i: (i, 0))
    return pl.pallas_call(
        functools.partial(_rmsnorm_kernel, transpose=transpose),
        grid=(s // tm,),
        in_specs=[pl.BlockSpec((tm, d), lambda i: (i, 0)),
                  pl.BlockSpec((1, d), lambda i: (0, 0))],
        out_specs=out_spec,
        out_shape=out_shape,
        compiler_params=_params(("parallel",)),
        name="rmsnorm_t" if transpose else "rmsnorm",
    )(x, g.reshape(1, d))


def _inproj_kernel(a_ref, b_ref, o_ref, w_ref):
    @pl.when(pl.program_id(1) == 0)
    def _():
        w_ref[...] = b_ref[...].astype(w_ref.dtype)

    acc = _dot(a_ref[...], w_ref[...])
    for c in range(o_ref.shape[0]):
        o_ref[c] = acc[:, c * LANES:(c + 1) * LANES].astype(o_ref.dtype)


def _inproj(xn, w, *, tm=512, tn=1024):
    s, d = xn.shape
    n = w.shape[1]
    return pl.pallas_call(
        _inproj_kernel,
        grid=(n // tn, s // tm),
        in_specs=[pl.BlockSpec((tm, d), lambda j, i: (i, 0)),
                  pl.BlockSpec((d, tn), lambda j, i: (0, j))],
        out_specs=pl.BlockSpec((tn // LANES, tm, LANES), lambda j, i: (j, i, 0)),
        out_shape=jax.ShapeDtypeStruct((n // LANES, s, LANES), jnp.bfloat16),
        scratch_shapes=[pltpu.VMEM((d, tn), jnp.bfloat16)],
        compiler_params=_params(("parallel", "arbitrary")),
        name="inproj",
    )(xn, w)


def _kmean_kernel(k_ref, o_ref):
    k = k_ref[0].astype(jnp.float32)
    nblk = k.shape[0] // MOBA_BLOCK
    km = jnp.mean(k.reshape(nblk, MOBA_BLOCK, LANES), axis=1)
    pad = jnp.zeros((LANES - nblk, LANES), jnp.float32)
    o_ref[0] = jnp.concatenate([km, pad], axis=0)


def _kmean(proj):
    s = proj.shape[1]
    return pl.pallas_call(
        _kmean_kernel,
        grid=(MOBA_HEADS,),
        in_specs=[pl.BlockSpec((1, s, LANES), lambda h: (SLAB_MK + h, 0, 0))],
        out_specs=pl.BlockSpec((1, LANES, LANES), lambda h: (h, 0, 0)),
        out_shape=jax.ShapeDtypeStruct((MOBA_HEADS, LANES, LANES), jnp.float32),
        compiler_params=_params(("parallel",)),
        name="moba_kmean",
    )(proj)


def _t5_bucket_starts():
    max_exact = REL_BUCKETS // 2
    d = np.arange(0, REL_MAX_DIST + 1)
    nf = np.maximum(d, 1).astype(np.float64)
    large = max_exact + (np.log(nf / max_exact) / math.log(REL_MAX_DIST / max_exact)
                         * (REL_BUCKETS - max_exact)).astype(np.int64)
    large = np.minimum(large, REL_BUCKETS - 1)
    bucket = np.where(d < max_exact, d, large)
    return [int(np.argmax(bucket >= b)) for b in range(REL_BUCKETS)]


_BUCKET_START = _t5_bucket_starts()


def _bias_kernel(rb_ref, o_ref):
    h = pl.program_id(0)
    qi = lax.broadcasted_iota(jnp.int32, (MOBA_BLOCK, MOBA_BLOCK), 0)
    ki = lax.broadcasted_iota(jnp.int32, (MOBA_BLOCK, MOBA_BLOCK), 1)
    for r in range(3):
        d = qi - ki + r * MOBA_BLOCK
        dd = jnp.maximum(d, 0)
        bias = jnp.full((MOBA_BLOCK, MOBA_BLOCK), rb_ref[0, h], jnp.float32)
        for b in range(1, REL_BUCKETS):
            bias = jnp.where(dd >= _BUCKET_START[b], rb_ref[b, h], bias)
        o_ref[0, r] = jnp.where(d >= 0, bias * LOG2E, NEG)


def _bias_tables(rel_bias):
    return pl.pallas_call(
        _bias_kernel,
        grid=(MOBA_HEADS,),
        in_specs=[pl.BlockSpec(memory_space=pltpu.SMEM)],
        out_specs=pl.BlockSpec((1, 3, MOBA_BLOCK, MOBA_BLOCK), lambda h: (h, 0, 0, 0)),
        out_shape=jax.ShapeDtypeStruct((MOBA_HEADS, 3, MOBA_BLOCK, MOBA_BLOCK), jnp.float32),
        compiler_params=_params(("arbitrary",)),
        name="moba_bias",
    )(rel_bias)


def _moba_kernel(q_ref, k_ref, v_ref, km_ref, tbl_ref, o_ref, s_ref, *, n_rows):
    qi = pl.program_id(1)
    blk = MOBA_BLOCK
    heads = q_ref.shape[0]
    score_scale = MOBA_HEAD_DIM ** -0.5 * LOG2E
    row = lax.broadcasted_iota(jnp.int32, (n_rows, blk), 0)
    rowf = row.astype(jnp.float32)
    valid = row < qi
    lane = lax.broadcasted_iota(jnp.int32, (blk, LANES), 1)
    never = jnp.full((LANES - n_rows, blk), NEG, jnp.float32)

    def augmented_query(hh):
        q = q_ref[hh]
        gate = _dot_nt(km_ref[hh].astype(jnp.bfloat16), q)[0:n_rows]
        g = jnp.where(valid, gate, NEG)
        sel = row == qi
        for _ in range(MOBA_TOPK):
            m = jnp.max(g, axis=0, keepdims=True)
            idx = jnp.min(jnp.where(g == m, rowf, float(n_rows)), axis=0, keepdims=True)
            pick = rowf == idx
            sel = sel | (pick & valid)
            g = jnp.where(pick, -jnp.inf, g)
        mask_t = jnp.concatenate([jnp.where(sel, 0.0, NEG), never], axis=0)
        mask_bias = mask_t.T.astype(jnp.bfloat16)
        return jnp.concatenate([q, mask_bias], axis=1)

    q_aug = [augmented_query(hh) for hh in range(heads)]

    grp = MOBA_BLOCKS_PER_ITER
    n_groups = (qi + grp) // grp

    def group_rows(t):
        js = [qi - grp * t - g for g in range(grp)]
        rows = [pl.ds(pl.multiple_of(jnp.maximum(j, 0) * blk, blk), blk) for j in js]
        return js, rows

    def fold(x):
        return [x[:, c * LANES:(c + 1) * LANES] for c in range(grp * blk // LANES)]

    def scores(t, mpart):
        js, rows = group_rows(t)
        hot = [(lane == jnp.where(j >= 0, j, LANES - 1)).astype(jnp.bfloat16) for j in js]
        rel = [jnp.minimum(grp * t + g, 2) for g in range(grp)]
        out = []
        for hh in range(heads):
            k_aug = jnp.concatenate(
                [jnp.concatenate([k_ref[hh, rows[g], :], hot[g]], axis=1) for g in range(grp)],
                axis=0)
            bias = jnp.concatenate([tbl_ref[hh, rel[g]] for g in range(grp)], axis=1)
            s = _dot_nt(q_aug[hh], k_aug) * score_scale + bias
            s_ref[hh, t] = s
            out.append(functools.reduce(jnp.maximum, fold(s), mpart[hh]))
        return tuple(out)

    mpart = lax.fori_loop(0, n_groups, scores,
                          tuple(jnp.full((blk, LANES), -jnp.inf, jnp.float32) for _ in range(heads)))
    m_row = [jnp.max(mp, axis=1, keepdims=True) for mp in mpart]

    def values(t, carry):
        _, rows = group_rows(t)
        out = []
        for hh in range(heads):
            lpart, acc = carry[hh]
            p = jnp.exp2(s_ref[hh, t] - m_row[hh])
            vv = jnp.concatenate([v_ref[hh, rows[g], :] for g in range(grp)], axis=0)
            out.append((functools.reduce(jnp.add, fold(p), lpart),
                        acc + _dot(p.astype(jnp.bfloat16), vv)))
        return tuple(out)

    zero = jnp.zeros((blk, LANES), jnp.float32)
    fin = lax.fori_loop(0, n_groups, values, tuple((zero, zero) for _ in range(heads)))
    for hh in range(heads):
        lpart, acc = fin[hh]
        denom = jnp.sum(lpart, axis=1, keepdims=True)
        o_ref[:, hh * LANES:(hh + 1) * LANES] = (acc / denom).astype(o_ref.dtype)


def _moba(proj, kmean, tables, *, heads=MOBA_HEADS_PER_STEP):
    s = proj.shape[1]
    nq = s // MOBA_BLOCK
    assert nq < LANES - 1 and MOBA_HEADS % heads == 0
    slab = lambda base: (lambda h, i: (base // heads + h, 0, 0))
    sub = 8
    return pl.pallas_call(
        functools.partial(_moba_kernel, n_rows=pl.cdiv(nq, sub) * sub),
        grid=(MOBA_HEADS // heads, nq),
        in_specs=[pl.BlockSpec((heads, MOBA_BLOCK, LANES), lambda h, i: (SLAB_MQ // heads + h, i, 0)),
                  pl.BlockSpec((heads, s, LANES), slab(SLAB_MK)),
                  pl.BlockSpec((heads, s, LANES), slab(SLAB_MV)),
                  pl.BlockSpec((heads, LANES, LANES), slab(0)),
                  pl.BlockSpec((heads, 3, MOBA_BLOCK, MOBA_BLOCK), lambda h, i: (h, 0, 0, 0))],
        out_specs=pl.BlockSpec((MOBA_BLOCK, heads * LANES), lambda h, i: (i, h)),
        out_shape=jax.ShapeDtypeStruct((s, MOBA_WIDTH), jnp.bfloat16),
        scratch_shapes=[pltpu.VMEM((heads, pl.cdiv(nq, MOBA_BLOCKS_PER_ITER), MOBA_BLOCK,
                                    MOBA_BLOCKS_PER_ITER * MOBA_BLOCK), jnp.float32)],
        compiler_params=_params(("parallel", "parallel")),
        name="moba_attn",
    )(proj, proj, proj, kmean, tables)


def _retention_kernel(lg_ref, q_ref, k_ref, v_ref, g_ref, cos_ref, sin_ref, o_ref,
                      state_ref, decay_ref):
    h = pl.program_id(0)
    c = pl.program_id(1)
    cs = RET_CHUNK
    lg = lg_ref[h]

    @pl.when(c == 0)
    def _():
        state_ref[...] = jnp.zeros_like(state_ref)
        i = lax.broadcasted_iota(jnp.int32, (cs, cs), 0)
        j = lax.broadcasted_iota(jnp.int32, (cs, cs), 1)
        diff = (i - j).astype(jnp.float32)
        decay_ref[...] = jnp.where(diff >= 0, jnp.exp(lg * jnp.maximum(diff, 0.0)), 0.0)

    cos = cos_ref[...]
    sin = sin_ref[...]

    def rope(ref):
        x1 = ref[0].astype(jnp.float32)
        x2 = ref[1].astype(jnp.float32)
        return jnp.concatenate([x1 * cos - x2 * sin, x1 * sin + x2 * cos], axis=1)

    rq = rope(q_ref)
    rk = rope(k_ref) * (RET_HEAD_DIM ** -0.5)
    v = jnp.concatenate([v_ref[0], v_ref[1]], axis=1)
    row = lax.broadcasted_iota(jnp.int32, (cs, 1), 0).astype(jnp.float32)
    q_decay = jnp.exp(lg * (row + 1.0))
    k_decay = jnp.exp(lg * (cs - 1.0 - row))

    state = state_ref[...]
    inner = _dot_nt(rq.astype(jnp.bfloat16), rk.astype(jnp.bfloat16)) * decay_ref[...]
    o = (_dot(inner.astype(jnp.bfloat16), v)
         + _dot((rq * q_decay).astype(jnp.bfloat16), state.astype(jnp.bfloat16)))
    kd_t = (rk * k_decay).T.astype(jnp.bfloat16)
    state_ref[...] = state * jnp.exp(lg * cs) + _dot(kd_t, v)

    mu = jnp.mean(o, axis=-1, keepdims=True)
    ctr = o - mu
    var = jnp.mean(ctr * ctr, axis=-1, keepdims=True)
    y = ctr * lax.rsqrt(var + GN_EPS)
    gate = jnp.concatenate([g_ref[0], g_ref[1]], axis=1).astype(jnp.float32)
    o_ref[...] = (gate * jax.nn.sigmoid(gate) * y).astype(o_ref.dtype)


def _retention(proj, log_g, cos, sin):
    s = proj.shape[1]
    cs = RET_CHUNK
    pair = lambda base: pl.BlockSpec((2, cs, LANES), lambda h, c: (base // 2 + h, c, 0))
    return pl.pallas_call(
        _retention_kernel,
        grid=(RET_HEADS, s // cs),
        in_specs=[pl.BlockSpec(memory_space=pltpu.SMEM),
                  pair(SLAB_RQ), pair(SLAB_RK), pair(SLAB_RV), pair(SLAB_RG),
                  pl.BlockSpec((cs, LANES), lambda h, c: (c, 0)),
                  pl.BlockSpec((cs, LANES), lambda h, c: (c, 0))],
        out_specs=pl.BlockSpec((cs, RET_HEAD_DIM), lambda h, c: (c, h)),
        out_shape=jax.ShapeDtypeStruct((s, RET_WIDTH), jnp.bfloat16),
        scratch_shapes=[pltpu.VMEM((RET_HEAD_DIM, RET_HEAD_DIM), jnp.float32),
                        pltpu.VMEM((cs, cs), jnp.float32)],
        compiler_params=_params(("parallel", "arbitrary")),
        name="retention",
    )(log_g, proj, proj, proj, proj, cos, sin)


def _outproj_kernel(a_ref, r_ref, wa_ref, wr_ref, x_ref, o_ref):
    o_ref[...] = x_ref[...] + _dot(a_ref[...], wa_ref[...]) + _dot(r_ref[...], wr_ref[...])


def _outproj(a_out, r_out, w_out, x, *, tm=512, tn=1024):
    s, d = x.shape
    ka = a_out.shape[1]
    kr = r_out.shape[1]
    return pl.pallas_call(
        _outproj_kernel,
        grid=(d // tn, s // tm),
        in_specs=[pl.BlockSpec((tm, ka), lambda j, i: (i, 0)),
                  pl.BlockSpec((tm, kr), lambda j, i: (i, 0)),
                  pl.BlockSpec((ka, tn), lambda j, i: (0, j)),
                  pl.BlockSpec((kr, tn), lambda j, i: (ka // kr, j)),
                  pl.BlockSpec((tm, tn), lambda j, i: (i, j))],
        out_specs=pl.BlockSpec((tm, tn), lambda j, i: (i, j)),
        out_shape=jax.ShapeDtypeStruct((s, d), jnp.float32),
        compiler_params=_params(("parallel", "parallel")),
        name="outproj",
    )(a_out, r_out, w_out, w_out, x)


def _sorting_network(n):
    def merge(lo, hi, r):
        step = r * 2
        if step < hi - lo:
            yield from merge(lo, hi, step)
            yield from merge(lo + r, hi, step)
            yield from ((i, i + r) for i in range(lo + r, hi - r, step))
        else:
            yield (lo, lo + r)

    def sort(lo, hi):
        if hi - lo >= 1:
            mid = lo + (hi - lo) // 2
            yield from sort(lo, mid)
            yield from sort(mid + 1, hi)
            yield from merge(lo, hi, 1)

    return list(sort(0, n - 1))


def _topk_desc(s, k):
    sub = 8
    groups = s.shape[0] // sub
    lists = [s[g * sub:(g + 1) * sub, :] for g in range(groups)]
    groups = max(1 << (groups - 1).bit_length(), k)
    lists += [jnp.full_like(lists[0], -jnp.inf)] * (groups - len(lists))
    for i, j in _sorting_network(groups):
        lists[i], lists[j] = jnp.maximum(lists[i], lists[j]), jnp.minimum(lists[i], lists[j])
    pos = lax.broadcasted_iota(jnp.int32, lists[0].shape, 0).astype(jnp.float32)
    out = []
    for r in range(k):
        head = lists[0]
        m = jnp.max(head, axis=0, keepdims=True)
        out.append(m)
        left = k - r - 1
        if left:
            first = jnp.min(jnp.where(head == m, pos, float(sub)), axis=0, keepdims=True)
            pick = pos == first
            for g in range(left):
                lists[g] = jnp.where(pick, lists[g + 1], lists[g])
    return out


def _peer_prep_kernel(xt_ref, wq_ref, keys_ref, s0_ref, s1_ref, thr_ref, nrm_ref, a_ref, b_ref):
    nk = PEER_NKEYS
    t = xt_ref.shape[1]
    qt = _dot(wq_ref[...], xt_ref[...]).astype(jnp.bfloat16)
    s0 = _dot(keys_ref[0], qt[:nk])
    s1 = _dot(keys_ref[1], qt[nk:])
    s0_ref[0] = s0
    s1_ref[0] = s1

    for i, m in enumerate(_topk_desc(jnp.concatenate([s0, s1], axis=1), PEER_TOPK)):
        a_ref[i:i + 1, :] = m[:, :t]
        b_ref[i:i + 1, :] = m[:, t:]
    a = a_ref[...]
    b = b_ref[...]
    half = PEER_TOPK // 2
    cands = [a[0:1] + b]
    for i in range(1, half):
        cands.append(a[i:i + 1] + b[0:half])
    cands.append(a[half:] + b[0:1])
    cand = jnp.concatenate(cands, axis=0)

    tops = _topk_desc(cand, PEER_TOPK)
    z = functools.reduce(jnp.add, [jnp.exp(m - tops[0]) for m in tops])
    thr_ref[0] = tops[-1]
    nrm_ref[0] = tops[0] + jnp.log(z)


def _peer_prep(xnt, wqt, keys, *, tt=512):
    d, t = xnt.shape
    nk = PEER_NKEYS
    sc_spec = pl.BlockSpec((1, nk, tt), lambda i, h: (h, 0, i))
    row_spec = pl.BlockSpec((1, 1, tt), lambda i, h: (h, 0, i))
    sc_shape = jax.ShapeDtypeStruct((PEER_HEADS, nk, t), jnp.float32)
    row_shape = jax.ShapeDtypeStruct((PEER_HEADS, 1, t), jnp.float32)
    return pl.pallas_call(
        _peer_prep_kernel,
        grid=(t // tt, PEER_HEADS),
        in_specs=[pl.BlockSpec((d, tt), lambda i, h: (0, i)),
                  pl.BlockSpec((PEER_QDIM, d), lambda i, h: (h, 0)),
                  pl.BlockSpec((2, nk, nk), lambda i, h: (h, 0, 0))],
        out_specs=[sc_spec, sc_spec, row_spec, row_spec],
        out_shape=[sc_shape, sc_shape, row_shape, row_shape],
        scratch_shapes=[pltpu.VMEM((PEER_TOPK, tt), jnp.float32),
                        pltpu.VMEM((PEER_TOPK, tt), jnp.float32)],
        compiler_params=_params(("parallel", "arbitrary")),
        name="peer_prep",
    )(xnt, wqt, keys)


def _peer_kernel(xt_ref, u_ref, vt_ref, s0_ref, s1_ref, thr_ref, nrm_ref, o_ref,
                 pre0_ref, pre1_ref, gw0_ref, gw1_ref, *, n_tiles):
    k = pl.program_id(1)
    nk = PEER_NKEYS
    te = pre0_ref.shape[0]
    rows = te // nk

    @pl.when(k == 0)
    def _():
        o_ref[...] = jnp.zeros_like(o_ref)
        pre1_ref[...] = jnp.zeros_like(pre1_ref)
        gw0_ref[...] = jnp.zeros_like(gw0_ref)

    def half_step(half, gate_tile, gw_in, pre_out, pre_in, gw_out):
        cols = slice(half * te, (half + 1) * te)
        o_ref[...] += _dot(vt_ref[:, cols], gw_in[...])
        pre_out[...] = _dot(u_ref[cols, :], xt_ref[...])
        for r in range(rows):
            rr = slice(r * nk, (r + 1) * nk)
            pre = pre_in[rr, :]
            act = 0.5 * pre * (1.0 + lax.erf(pre * math.sqrt(0.5)))
            w = jnp.zeros_like(pre)
            for h in range(PEER_HEADS):
                z = s0_ref[h, pl.ds(gate_tile * rows + r, 1), :] + s1_ref[h]
                w = w + jnp.where(z >= thr_ref[h], jnp.exp(z - nrm_ref[h]), 0.0)
            gw_out[rr, :] = (act * w).astype(gw_out.dtype)

    last = n_tiles - 1
    half_step(0, jnp.clip(2 * k - 1, 0, last), gw0_ref, pre0_ref, pre1_ref, gw1_ref)
    half_step(1, jnp.minimum(2 * k, last), gw1_ref, pre1_ref, pre0_ref, gw0_ref)


def _peer(xnt, u, vt, s0, s1, thr, nrm, *, tt=512, te=256):
    d, t = xnt.shape
    e = u.shape[0]
    nk = PEER_NKEYS
    n_tiles = e // te
    n_steps = n_tiles // 2 + 1
    full = pl.BlockSpec((PEER_HEADS, nk, tt), lambda i, k: (0, 0, i))
    rowv = pl.BlockSpec((PEER_HEADS, 1, tt), lambda i, k: (0, 0, i))
    return pl.pallas_call(
        functools.partial(_peer_kernel, n_tiles=n_tiles),
        grid=(t // tt, n_steps),
        in_specs=[pl.BlockSpec((d, tt), lambda i, k: (0, i)),
                  pl.BlockSpec((2 * te, d), lambda i, k: (jnp.minimum(k, n_steps - 2), 0)),
                  pl.BlockSpec((d, 2 * te), lambda i, k: (0, jnp.maximum(k - 1, 0))),
                  full, full, rowv, rowv],
        out_specs=pl.BlockSpec((d, tt), lambda i, k: (0, i)),
        out_shape=jax.ShapeDtypeStruct((d, t), jnp.float32),
        scratch_shapes=[pltpu.VMEM((te, tt), jnp.float32), pltpu.VMEM((te, tt), jnp.float32),
                        pltpu.VMEM((te, tt), jnp.bfloat16), pltpu.VMEM((te, tt), jnp.bfloat16)],
        compiler_params=_params(("parallel", "arbitrary")),
        name="peer_experts",
    )(xnt, u, vt, s0, s1, thr, nrm)


def _final_kernel(h_ref, pt_ref, g_ref, o_ref):
    x = h_ref[...] + pt_ref[...].T
    o_ref[...] = x * lax.rsqrt(jnp.mean(x * x, axis=-1, keepdims=True) + NORM_EPS) * g_ref[...]


def _final(h, peer_t, g, *, tm=256):
    s, d = h.shape
    return pl.pallas_call(
        _final_kernel,
        grid=(s // tm,),
        in_specs=[pl.BlockSpec((tm, d), lambda i: (i, 0)),
                  pl.BlockSpec((d, tm), lambda i: (0, i)),
                  pl.BlockSpec((1, d), lambda i: (0, 0))],
        out_specs=pl.BlockSpec((tm, d), lambda i: (i, 0)),
        out_shape=jax.ShapeDtypeStruct((s, d), jnp.float32),
        compiler_params=_params(("parallel",)),
        name="final_norm",
    )(h, peer_t, g.reshape(1, d))


def kernel(x, norm_mix_g, w_in, w_out, rel_bias, norm_ffn_g, peer_w_q, peer_sub_keys,
           peer_u, peer_v, norm_final_g):
    b, s, d = x.shape
    assert b == 1 and d == D_MODEL and s % RET_CHUNK == 0 and norm_mix_g.shape[0] == 1
    bf16 = jnp.bfloat16
    h = x.reshape(s, d)

    half = RET_HEAD_DIM // 2
    inv = ROPE_BASE ** (-jnp.arange(half, dtype=jnp.float32) / half)
    ang = jnp.arange(s, dtype=jnp.int32).astype(jnp.float32)[:, None] * inv[None, :]
    cos, sin = jnp.cos(ang), jnp.sin(ang)
    log_g = jnp.log1p(-(2.0 ** (-5.0 - jnp.arange(RET_HEADS, dtype=jnp.float32))))

    xn = _rmsnorm(h, norm_mix_g[0], transpose=False)
    proj = _inproj(xn, w_in[0])
    a_out = _moba(proj, _kmean(proj), _bias_tables(rel_bias))
    r_out = _retention(proj, log_g, cos, sin)
    h = _outproj(a_out, r_out, w_out[0].astype(bf16), h)

    xnt = _rmsnorm(h, norm_ffn_g[0], transpose=True)
    keys = peer_sub_keys[0].astype(bf16).reshape(PEER_HEADS * 2, PEER_NKEYS, PEER_QDIM // 2)
    s0, s1, thr, nrm = _peer_prep(xnt, peer_w_q[0].T.astype(bf16), keys)
    peer_t = _peer(xnt, peer_u[0].astype(bf16), peer_v[0].T.astype(bf16), s0, s1, thr, nrm)
    y = _final(h, peer_t, norm_final_g)
    return y.reshape(b, s, d)
```

```python
import functools
import math

import numpy as np
import jax
import jax.numpy as jnp
from jax import lax
from jax.experimental import pallas as pl
from jax.experimental.pallas import tpu as pltpu

D_MODEL = 4096
MOBA_HEADS = 16
MOBA_HEAD_DIM = 128
MOBA_WIDTH = MOBA_HEADS * MOBA_HEAD_DIM
MOBA_BLOCK = 256
MOBA_TOPK = 3
RET_HEADS = 8
RET_HEAD_DIM = 256
RET_WIDTH = RET_HEADS * RET_HEAD_DIM
RET_CHUNK = 512
ROPE_BASE = 10000.0
REL_BUCKETS = 32
REL_MAX_DIST = 128
IN_WIDTH = 3 * MOBA_WIDTH + 4 * RET_WIDTH
PEER_HEADS = 8
PEER_NKEYS = 128
PEER_EXPERTS = PEER_NKEYS * PEER_NKEYS
PEER_QDIM = 256
PEER_TOPK = 16
NORM_EPS = 1e-6
GN_EPS = 1e-6
NEG = -1e30

LOG2E = math.log2(math.e)
MOBA_HEADS_PER_STEP = 2
MOBA_BLOCKS_PER_ITER = 8
LANES = 128
N_SLABS = IN_WIDTH // LANES
VMEM_LIMIT = 56 * 1024 * 1024

SLAB_MQ = 0
SLAB_MK = MOBA_WIDTH // LANES
SLAB_MV = 2 * MOBA_WIDTH // LANES
SLAB_RQ = 3 * MOBA_WIDTH // LANES
SLAB_RK = SLAB_RQ + RET_WIDTH // LANES
SLAB_RV = SLAB_RK + RET_WIDTH // LANES
SLAB_RG = SLAB_RV + RET_WIDTH // LANES


def _params(sem, vmem=VMEM_LIMIT):
    return pltpu.CompilerParams(dimension_semantics=sem, vmem_limit_bytes=vmem)


def _dot(a, b):
    return jnp.dot(a, b, preferred_element_type=jnp.float32)


def _dot_nt(a, b):
    return lax.dot_general(a, b, (((1,), (1,)), ((), ())), preferred_element_type=jnp.float32)


def _rmsnorm_kernel(x_ref, g_ref, o_ref, *, transpose):
    x = x_ref[...]
    y = x * lax.rsqrt(jnp.mean(x * x, axis=-1, keepdims=True) + NORM_EPS) * g_ref[...]
    if transpose:
        y = y.T
    o_ref[...] = y.astype(o_ref.dtype)


def _rmsnorm(x, g, *, transpose, tm=256):
    s, d = x.shape
    if transpose:
        out_shape = jax.ShapeDtypeStruct((d, s), jnp.bfloat16)
        out_spec = pl.BlockSpec((d, tm), lambda i: (0, i))
    else:
        out_shape = jax.ShapeDtypeStruct((s, d), jnp.bfloat16)
        out_spec = pl.BlockSpec((tm, d), lambda i: (i, 0))
    return pl.pallas_call(
        functools.partial(_rmsnorm_kernel, transpose=transpose),
        grid=(s // tm,),
        in_specs=[pl.BlockSpec((tm, d), lambda i: (i, 0)),
                  pl.BlockSpec((1, d), lambda i: (0, 0))],
        out_specs=out_spec,
        out_shape=out_shape,
        compiler_params=_params(("parallel",)),
        name="rmsnorm_t" if transpose else "rmsnorm",
    )(x, g.reshape(1, d))


def _inproj_kernel(a_ref, b_ref, o_ref, w_ref):
    @pl.when(pl.program_id(1) == 0)
    def _():
        w_ref[...] = b_ref[...].astype(w_ref.dtype)

    acc = _dot(a_ref[...], w_ref[...])
    for c in range(o_ref.shape[0]):
        o_ref[c] = acc[:, c * LANES:(c + 1) * LANES].astype(o_ref.dtype)


def _inproj(xn, w, *, tm=512, tn=1024):
    s, d = xn.shape
    n = w.shape[1]
    return pl.pallas_call(
        _inproj_kernel,
        grid=(n // tn, s // tm),
        in_specs=[pl.BlockSpec((tm, d), lambda j, i: (i, 0)),
                  pl.BlockSpec((d, tn), lambda j, i: (0, j))],
        out_specs=pl.BlockSpec((tn // LANES, tm, LANES), lambda j, i: (j, i, 0)),
        out_shape=jax.ShapeDtypeStruct((n // LANES, s, LANES), jnp.bfloat16),
        scratch_shapes=[pltpu.VMEM((d, tn), jnp.bfloat16)],
        compiler_params=_params(("parallel", "arbitrary")),
        name="inproj",
    )(xn, w)


def _kmean_kernel(k_ref, o_ref):
    k = k_ref[0].astype(jnp.float32)
    nblk = k.shape[0] // MOBA_BLOCK
    km = jnp.mean(k.reshape(nblk, MOBA_BLOCK, LANES), axis=1)
    pad = jnp.zeros((LANES - nblk, LANES), jnp.float32)
    o_ref[0] = jnp.concatenate([km, pad], axis=0)


def _kmean(proj):
    s = proj.shape[1]
    return pl.pallas_call(
        _kmean_kernel,
        grid=(MOBA_HEADS,),
        in_specs=[pl.BlockSpec((1, s, LANES), lambda h: (SLAB_MK + h, 0, 0))],
        out_specs=pl.BlockSpec((1, LANES, LANES), lambda h: (h, 0, 0)),
        out_shape=jax.ShapeDtypeStruct((MOBA_HEADS, LANES, LANES), jnp.float32),
        compiler_params=_params(("parallel",)),
        name="moba_kmean",
    )(proj)


def _t5_bucket_starts():
    max_exact = REL_BUCKETS // 2
    d = np.arange(0, REL_MAX_DIST + 1)
    nf = np.maximum(d, 1).astype(np.float64)
    large = max_exact + (np.log(nf / max_exact) / math.log(REL_MAX_DIST / max_exact)
                         * (REL_BUCKETS - max_exact)).astype(np.int64)
    large = np.minimum(large, REL_BUCKETS - 1)
    bucket = np.where(d < max_exact, d, large)
    return [int(np.argmax(bucket >= b)) for b in range(REL_BUCKETS)]


_BUCKET_START = _t5_bucket_starts()


def _bias_kernel(rb_ref, o_ref):
    h = pl.program_id(0)
    qi = lax.broadcasted_iota(jnp.int32, (MOBA_BLOCK, MOBA_BLOCK), 0)
    ki = lax.broadcasted_iota(jnp.int32, (MOBA_BLOCK, MOBA_BLOCK), 1)
    for r in range(3):
        d = qi - ki + r * MOBA_BLOCK
        dd = jnp.maximum(d, 0)
        bias = jnp.full((MOBA_BLOCK, MOBA_BLOCK), rb_ref[0, h], jnp.float32)
        for b in range(1, REL_BUCKETS):
            bias = jnp.where(dd >= _BUCKET_START[b], rb_ref[b, h], bias)
        o_ref[0, r] = jnp.where(d >= 0, bias * LOG2E, NEG)


def _bias_tables(rel_bias):
    return pl.pallas_call(
        _bias_kernel,
        grid=(MOBA_HEADS,),
        in_specs=[pl.BlockSpec(memory_space=pltpu.SMEM)],
        out_specs=pl.BlockSpec((1, 3, MOBA_BLOCK, MOBA_BLOCK), lambda h: (h, 0, 0, 0)),
        out_shape=jax.ShapeDtypeStruct((MOBA_HEADS, 3, MOBA_BLOCK, MOBA_BLOCK), jnp.float32),
        compiler_params=_params(("arbitrary",)),
        name="moba_bias",
    )(rel_bias)


def _moba_kernel(q_ref, k_ref, v_ref, km_ref, tbl_ref, o_ref, s_ref, *, n_rows):
    qi = pl.program_id(1)
    blk = MOBA_BLOCK
    heads = q_ref.shape[0]
    score_scale = MOBA_HEAD_DIM ** -0.5 * LOG2E
    row = lax.broadcasted_iota(jnp.int32, (n_rows, blk), 0)
    rowf = row.astype(jnp.float32)
    valid = row < qi
    lane = lax.broadcasted_iota(jnp.int32, (blk, LANES), 1)
    never = jnp.full((LANES - n_rows, blk), NEG, jnp.float32)

    def augmented_query(hh):
        q = q_ref[hh]
        gate = _dot_nt(km_ref[hh].astype(jnp.bfloat16), q)[0:n_rows]
        g = jnp.where(valid, gate, NEG)
        sel = row == qi
        for _ in range(MOBA_TOPK):
            m = jnp.max(g, axis=0, keepdims=True)
            idx = jnp.min(jnp.where(g == m, rowf, float(n_rows)), axis=0, keepdims=True)
            pick = rowf == idx
            sel = sel | (pick & valid)
            g = jnp.where(pick, -jnp.inf, g)
        mask_t = jnp.concatenate([jnp.where(sel, 0.0, NEG), never], axis=0)
        mask_bias = mask_t.T.astype(jnp.bfloat16)
        return jnp.concatenate([q, mask_bias], axis=1)

    q_aug = [augmented_query(hh) for hh in range(heads)]

    grp = MOBA_BLOCKS_PER_ITER
    n_groups = (qi + grp) // grp

    def group_rows(t):
        js = [qi - grp * t - g for g in range(grp)]
        rows = [pl.ds(pl.multiple_of(jnp.maximum(j, 0) * blk, blk), blk) for j in js]
        return js, rows

    def fold(x):
        return [x[:, c * LANES:(c + 1) * LANES] for c in range(grp * blk // LANES)]

    def scores(t, mpart):
        js, rows = group_rows(t)
        hot = [(lane == jnp.where(j >= 0, j, LANES - 1)).astype(jnp.bfloat16) for j in js]
        rel = [jnp.minimum(grp * t + g, 2) for g in range(grp)]
        out = []
        for hh in range(heads):
            k_aug = jnp.concatenate(
                [jnp.concatenate([k_ref[hh, rows[g], :], hot[g]], axis=1) for g in range(grp)],
                axis=0)
            bias = jnp.concatenate([tbl_ref[hh, rel[g]] for g in range(grp)], axis=1)
            s = _dot_nt(q_aug[hh], k_aug) * score_scale + bias
            s_ref[hh, t] = s
            out.append(functools.reduce(jnp.maximum, fold(s), mpart[hh]))
        return tuple(out)

    mpart = lax.fori_loop(0, n_groups, scores,
                          tuple(jnp.full((blk, LANES), -jnp.inf, jnp.float32) for _ in range(heads)))
    m_row = [jnp.max(mp, axis=1, keepdims=True) for mp in mpart]

    ones = jnp.ones((grp * blk, LANES), jnp.bfloat16)

    def values(t, carry):
        _, rows = group_rows(t)
        out = []
        for hh in range(heads):
            p = jnp.exp2(s_ref[hh, t] - m_row[hh])
            vv = jnp.concatenate([v_ref[hh, rows[g], :] for g in range(grp)], axis=0)
            out.append(carry[hh] + _dot(p.astype(jnp.bfloat16), jnp.concatenate([vv, ones], axis=1)))
        return tuple(out)

    fin = lax.fori_loop(0, n_groups, values,
                        tuple(jnp.zeros((blk, 2 * LANES), jnp.float32) for _ in range(heads)))
    for hh in range(heads):
        acc = fin[hh]
        o_ref[:, hh * LANES:(hh + 1) * LANES] = (acc[:, :LANES] / acc[:, LANES:LANES + 1]).astype(o_ref.dtype)


def _moba(proj, kmean, tables, *, heads=MOBA_HEADS_PER_STEP):
    s = proj.shape[1]
    nq = s // MOBA_BLOCK
    assert nq < LANES - 1 and MOBA_HEADS % heads == 0
    slab = lambda base: (lambda h, i: (base // heads + h, 0, 0))
    sub = 8
    return pl.pallas_call(
        functools.partial(_moba_kernel, n_rows=pl.cdiv(nq, sub) * sub),
        grid=(MOBA_HEADS // heads, nq),
        in_specs=[pl.BlockSpec((heads, MOBA_BLOCK, LANES), lambda h, i: (SLAB_MQ // heads + h, i, 0)),
                  pl.BlockSpec((heads, s, LANES), slab(SLAB_MK)),
                  pl.BlockSpec((heads, s, LANES), slab(SLAB_MV)),
                  pl.BlockSpec((heads, LANES, LANES), slab(0)),
                  pl.BlockSpec((heads, 3, MOBA_BLOCK, MOBA_BLOCK), lambda h, i: (h, 0, 0, 0))],
        out_specs=pl.BlockSpec((MOBA_BLOCK, heads * LANES), lambda h, i: (i, h)),
        out_shape=jax.ShapeDtypeStruct((s, MOBA_WIDTH), jnp.bfloat16),
        scratch_shapes=[pltpu.VMEM((heads, pl.cdiv(nq, MOBA_BLOCKS_PER_ITER), MOBA_BLOCK,
                                    MOBA_BLOCKS_PER_ITER * MOBA_BLOCK), jnp.float32)],
        compiler_params=_params(("parallel", "parallel")),
        name="moba_attn",
    )(proj, proj, proj, kmean, tables)


def _retention_kernel(lg_ref, q_ref, k_ref, v_ref, g_ref, cos_ref, sin_ref, o_ref,
                      state_ref, decay_ref):
    h = pl.program_id(0)
    c = pl.program_id(1)
    cs = RET_CHUNK
    lg = lg_ref[h]

    @pl.when(c == 0)
    def _():
        state_ref[...] = jnp.zeros_like(state_ref)
        i = lax.broadcasted_iota(jnp.int32, (cs, cs), 0)
        j = lax.broadcasted_iota(jnp.int32, (cs, cs), 1)
        diff = (i - j).astype(jnp.float32)
        decay_ref[...] = jnp.where(diff >= 0, jnp.exp(lg * jnp.maximum(diff, 0.0)), 0.0)

    cos = cos_ref[...]
    sin = sin_ref[...]

    def rope(ref):
        x1 = ref[0].astype(jnp.float32)
        x2 = ref[1].astype(jnp.float32)
        return jnp.concatenate([x1 * cos - x2 * sin, x1 * sin + x2 * cos], axis=1)

    rq = rope(q_ref)
    rk = rope(k_ref) * (RET_HEAD_DIM ** -0.5)
    v = jnp.concatenate([v_ref[0], v_ref[1]], axis=1)
    row = lax.broadcasted_iota(jnp.int32, (cs, 1), 0).astype(jnp.float32)
    q_decay = jnp.exp(lg * (row + 1.0))
    k_decay = jnp.exp(lg * (cs - 1.0 - row))

    state = state_ref[...]
    inner = _dot_nt(rq.astype(jnp.bfloat16), rk.astype(jnp.bfloat16)) * decay_ref[...]
    o = (_dot(inner.astype(jnp.bfloat16), v)
         + _dot((rq * q_decay).astype(jnp.bfloat16), state.astype(jnp.bfloat16)))
    kd_t = (rk * k_decay).T.astype(jnp.bfloat16)
    state_ref[...] = state * jnp.exp(lg * cs) + _dot(kd_t, v)

    mu = jnp.mean(o, axis=-1, keepdims=True)
    ctr = o - mu
    var = jnp.mean(ctr * ctr, axis=-1, keepdims=True)
    y = ctr * lax.rsqrt(var + GN_EPS)
    gate = jnp.concatenate([g_ref[0], g_ref[1]], axis=1).astype(jnp.float32)
    o_ref[...] = (gate * jax.nn.sigmoid(gate) * y).astype(o_ref.dtype)


def _retention(proj, log_g, cos, sin):
    s = proj.shape[1]
    cs = RET_CHUNK
    pair = lambda base: pl.BlockSpec((2, cs, LANES), lambda h, c: (base // 2 + h, c, 0))
    return pl.pallas_call(
        _retention_kernel,
        grid=(RET_HEADS, s // cs),
        in_specs=[pl.BlockSpec(memory_space=pltpu.SMEM),
                  pair(SLAB_RQ), pair(SLAB_RK), pair(SLAB_RV), pair(SLAB_RG),
                  pl.BlockSpec((cs, LANES), lambda h, c: (c, 0)),
                  pl.BlockSpec((cs, LANES), lambda h, c: (c, 0))],
        out_specs=pl.BlockSpec((cs, RET_HEAD_DIM), lambda h, c: (c, h)),
        out_shape=jax.ShapeDtypeStruct((s, RET_WIDTH), jnp.bfloat16),
        scratch_shapes=[pltpu.VMEM((RET_HEAD_DIM, RET_HEAD_DIM), jnp.float32),
                        pltpu.VMEM((cs, cs), jnp.float32)],
        compiler_params=_params(("parallel", "arbitrary")),
        name="retention",
    )(log_g, proj, proj, proj, proj, cos, sin)


def _outproj_kernel(a_ref, r_ref, wa_ref, wr_ref, x_ref, o_ref):
    o_ref[...] = x_ref[...] + _dot(a_ref[...], wa_ref[...]) + _dot(r_ref[...], wr_ref[...])


def _outproj(a_out, r_out, w_out, x, *, tm=512, tn=1024):
    s, d = x.shape
    ka = a_out.shape[1]
    kr = r_out.shape[1]
    return pl.pallas_call(
        _outproj_kernel,
        grid=(d // tn, s // tm),
        in_specs=[pl.BlockSpec((tm, ka), lambda j, i: (i, 0)),
                  pl.BlockSpec((tm, kr), lambda j, i: (i, 0)),
                  pl.BlockSpec((ka, tn), lambda j, i: (0, j)),
                  pl.BlockSpec((kr, tn), lambda j, i: (ka // kr, j)),
                  pl.BlockSpec((tm, tn), lambda j, i: (i, j))],
        out_specs=pl.BlockSpec((tm, tn), lambda j, i: (i, j)),
        out_shape=jax.ShapeDtypeStruct((s, d), jnp.float32),
        compiler_params=_params(("parallel", "parallel")),
        name="outproj",
    )(a_out, r_out, w_out, w_out, x)


def _sorting_network(n):
    def merge(lo, hi, r):
        step = r * 2
        if step < hi - lo:
            yield from merge(lo, hi, step)
            yield from merge(lo + r, hi, step)
            yield from ((i, i + r) for i in range(lo + r, hi - r, step))
        else:
            yield (lo, lo + r)

    def sort(lo, hi):
        if hi - lo >= 1:
            mid = lo + (hi - lo) // 2
            yield from sort(lo, mid)
            yield from sort(mid + 1, hi)
            yield from merge(lo, hi, 1)

    return list(sort(0, n - 1))


def _topk_desc(s, k):
    sub = 8
    groups = s.shape[0] // sub
    lists = [s[g * sub:(g + 1) * sub, :] for g in range(groups)]
    groups = max(1 << (groups - 1).bit_length(), k)
    lists += [jnp.full_like(lists[0], -jnp.inf)] * (groups - len(lists))
    for i, j in _sorting_network(groups):
        lists[i], lists[j] = jnp.maximum(lists[i], lists[j]), jnp.minimum(lists[i], lists[j])
    pos = lax.broadcasted_iota(jnp.int32, lists[0].shape, 0).astype(jnp.float32)
    out = []
    for r in range(k):
        head = lists[0]
        m = jnp.max(head, axis=0, keepdims=True)
        out.append(m)
        left = k - r - 1
        if left:
            first = jnp.min(jnp.where(head == m, pos, float(sub)), axis=0, keepdims=True)
            pick = pos == first
            for g in range(left):
                lists[g] = jnp.where(pick, lists[g + 1], lists[g])
    return out


def _peer_prep_kernel(xt_ref, wq_ref, keys_ref, s0_ref, s1_ref, thr_ref, nrm_ref, a_ref, b_ref):
    nk = PEER_NKEYS
    t = xt_ref.shape[1]
    qt = _dot(wq_ref[...], xt_ref[...]).astype(jnp.bfloat16)
    s0 = _dot(keys_ref[0], qt[:nk])
    s1 = _dot(keys_ref[1], qt[nk:])
    s0_ref[0] = s0
    s1_ref[0] = s1

    for i, m in enumerate(_topk_desc(jnp.concatenate([s0, s1], axis=1), PEER_TOPK)):
        a_ref[i:i + 1, :] = m[:, :t]
        b_ref[i:i + 1, :] = m[:, t:]
    a = a_ref[...]
    b = b_ref[...]
    half = PEER_TOPK // 2
    cands = [a[0:1] + b]
    for i in range(1, half):
        cands.append(a[i:i + 1] + b[0:half])
    cands.append(a[half:] + b[0:1])
    cand = jnp.concatenate(cands, axis=0)

    tops = _topk_desc(cand, PEER_TOPK)
    z = functools.reduce(jnp.add, [jnp.exp(m - tops[0]) for m in tops])
    thr_ref[0] = tops[-1]
    nrm_ref[0] = tops[0] + jnp.log(z)


def _peer_prep(xnt, wqt, keys, *, tt=512):
    d, t = xnt.shape
    nk = PEER_NKEYS
    sc_spec = pl.BlockSpec((1, nk, tt), lambda i, h: (h, 0, i))
    row_spec = pl.BlockSpec((1, 1, tt), lambda i, h: (h, 0, i))
    sc_shape = jax.ShapeDtypeStruct((PEER_HEADS, nk, t), jnp.float32)
    row_shape = jax.ShapeDtypeStruct((PEER_HEADS, 1, t), jnp.float32)
    return pl.pallas_call(
        _peer_prep_kernel,
        grid=(t // tt, PEER_HEADS),
        in_specs=[pl.BlockSpec((d, tt), lambda i, h: (0, i)),
                  pl.BlockSpec((PEER_QDIM, d), lambda i, h: (h, 0)),
                  pl.BlockSpec((2, nk, nk), lambda i, h: (h, 0, 0))],
        out_specs=[sc_spec, sc_spec, row_spec, row_spec],
        out_shape=[sc_shape, sc_shape, row_shape, row_shape],
        scratch_shapes=[pltpu.VMEM((PEER_TOPK, tt), jnp.float32),
                        pltpu.VMEM((PEER_TOPK, tt), jnp.float32)],
        compiler_params=_params(("parallel", "arbitrary")),
        name="peer_prep",
    )(xnt, wqt, keys)


def _peer_kernel(xt_ref, u_ref, vt_ref, s0_ref, s1_ref, thr_ref, nrm_ref, o_ref,
                 pre0_ref, pre1_ref, gw0_ref, gw1_ref, *, n_tiles):
    k = pl.program_id(1)
    nk = PEER_NKEYS
    te = pre0_ref.shape[0]
    rows = te // nk

    @pl.when(k == 0)
    def _():
        o_ref[...] = jnp.zeros_like(o_ref)
        pre1_ref[...] = jnp.zeros_like(pre1_ref)
        gw0_ref[...] = jnp.zeros_like(gw0_ref)

    def half_step(half, gate_tile, gw_in, pre_out, pre_in, gw_out):
        cols = slice(half * te, (half + 1) * te)
        o_ref[...] += _dot(vt_ref[:, cols], gw_in[...])
        pre_out[...] = _dot(u_ref[cols, :], xt_ref[...])
        for r in range(rows):
            rr = slice(r * nk, (r + 1) * nk)
            pre = pre_in[rr, :]
            act = 0.5 * pre * (1.0 + lax.erf(pre * math.sqrt(0.5)))
            w = jnp.zeros_like(pre)
            for h in range(PEER_HEADS):
                z = s0_ref[h, pl.ds(gate_tile * rows + r, 1), :] + s1_ref[h]
                w = w + jnp.where(z >= thr_ref[h], jnp.exp(z - nrm_ref[h]), 0.0)
            gw_out[rr, :] = (act * w).astype(gw_out.dtype)

    last = n_tiles - 1
    half_step(0, jnp.clip(2 * k - 1, 0, last), gw0_ref, pre0_ref, pre1_ref, gw1_ref)
    half_step(1, jnp.minimum(2 * k, last), gw1_ref, pre1_ref, pre0_ref, gw0_ref)


def _peer(xnt, u, vt, s0, s1, thr, nrm, *, tt=512, te=256):
    d, t = xnt.shape
    e = u.shape[0]
    nk = PEER_NKEYS
    n_tiles = e // te
    n_steps = n_tiles // 2 + 1
    full = pl.BlockSpec((PEER_HEADS, nk, tt), lambda i, k: (0, 0, i))
    rowv = pl.BlockSpec((PEER_HEADS, 1, tt), lambda i, k: (0, 0, i))
    return pl.pallas_call(
        functools.partial(_peer_kernel, n_tiles=n_tiles),
        grid=(t // tt, n_steps),
        in_specs=[pl.BlockSpec((d, tt), lambda i, k: (0, i)),
                  pl.BlockSpec((2 * te, d), lambda i, k: (jnp.minimum(k, n_steps - 2), 0)),
                  pl.BlockSpec((d, 2 * te), lambda i, k: (0, jnp.maximum(k - 1, 0))),
                  full, full, rowv, rowv],
        out_specs=pl.BlockSpec((d, tt), lambda i, k: (0, i)),
        out_shape=jax.ShapeDtypeStruct((d, t), jnp.float32),
        scratch_shapes=[pltpu.VMEM((te, tt), jnp.float32), pltpu.VMEM((te, tt), jnp.float32),
                        pltpu.VMEM((te, tt), jnp.bfloat16), pltpu.VMEM((te, tt), jnp.bfloat16)],
        compiler_params=_params(("parallel", "arbitrary")),
        name="peer_experts",
    )(xnt, u, vt, s0, s1, thr, nrm)


def _final_kernel(h_ref, pt_ref, g_ref, o_ref):
    x = h_ref[...] + pt_ref[...].T
    o_ref[...] = x * lax.rsqrt(jnp.mean(x * x, axis=-1, keepdims=True) + NORM_EPS) * g_ref[...]


def _final(h, peer_t, g, *, tm=256):
    s, d = h.shape
    return pl.pallas_call(
        _final_kernel,
        grid=(s // tm,),
        in_specs=[pl.BlockSpec((tm, d), lambda i: (i, 0)),
                  pl.BlockSpec((d, tm), lambda i: (0, i)),
                  pl.BlockSpec((1, d), lambda i: (0, 0))],
        out_specs=pl.BlockSpec((tm, d), lambda i: (i, 0)),
        out_shape=jax.ShapeDtypeStruct((s, d), jnp.float32),
        compiler_params=_params(("parallel",)),
        name="final_norm",
    )(h, peer_t, g.reshape(1, d))


def kernel(x, norm_mix_g, w_in, w_out, rel_bias, norm_ffn_g, peer_w_q, peer_sub_keys,
           peer_u, peer_v, norm_final_g):
    b, s, d = x.shape
    assert b == 1 and d == D_MODEL and s % RET_CHUNK == 0 and norm_mix_g.shape[0] == 1
    bf16 = jnp.bfloat16
    h = x.reshape(s, d)

    half = RET_HEAD_DIM // 2
    inv = ROPE_BASE ** (-jnp.arange(half, dtype=jnp.float32) / half)
    ang = jnp.arange(s, dtype=jnp.int32).astype(jnp.float32)[:, None] * inv[None, :]
    cos, sin = jnp.cos(ang), jnp.sin(ang)
    log_g = jnp.log1p(-(2.0 ** (-5.0 - jnp.arange(RET_HEADS, dtype=jnp.float32))))

    xn = _rmsnorm(h, norm_mix_g[0], transpose=False)
    proj = _inproj(xn, w_in[0])
    a_out = _moba(proj, _kmean(proj), _bias_tables(rel_bias))
    r_out = _retention(proj, log_g, cos, sin)
    h = _outproj(a_out, r_out, w_out[0].astype(bf16), h)

    xnt = _rmsnorm(h, norm_ffn_g[0], transpose=True)
    keys = peer_sub_keys[0].astype(bf16).reshape(PEER_HEADS * 2, PEER_NKEYS, PEER_QDIM // 2)
    s0, s1, thr, nrm = _peer_prep(xnt, peer_w_q[0].T.astype(bf16), keys)
    peer_t = _peer(xnt, peer_u[0].astype(bf16), peer_v[0].T.astype(bf16), s0, s1, thr, nrm)
    y = _final(h, peer_t, norm_final_g)
    return y.reshape(b, s, d)
```
